```python
import jax, jax.numpy as jnp
from jax import lax
import numpy as np

D_MODEL = 1024
BATCH = 4
SEQ = 8192
DEPTH = 2

GRID_W = 64
CTX_LEN = 256
N_HEADS = 16
N_KV_HEADS = 4
HEAD_DIM = 64
Q_W = N_HEADS * HEAD_DIM
KV_W = N_KV_HEADS * HEAD_DIM
WINDOW = 128
BLK = 128
ROPE_THETA = 10000.0
CHUNK = 128
A_W = D_MODEL
A_GROUPS = 8
A_GW = A_W // A_GROUPS
B_W = D_MODEL
CONV_W = 3
N_BRANCH = 3
BRANCH_W = D_MODEL
D_FF = 2816
EPS = 1e-6
NEG = -1e30
OFF_Q = 0
OFF_K = OFF_Q + Q_W
OFF_V = OFF_K + KV_W
OFF_A = OFF_V + KV_W
OFF_B = OFF_A + 2 * A_W
OFF_G = OFF_B + 3 * B_W
IN_W = OFF_G + N_BRANCH * D_MODEL

kernel_name = "hybrid_gated_parallel_dit_block"


def rmsnorm(x, g):
    xf = x.astype(jnp.float32)
    y = xf * lax.rsqrt(jnp.mean(xf * xf, axis=-1, keepdims=True) + EPS)
    return (y * g.astype(jnp.float32)).astype(x.dtype)


def modulate(h, shift, scale):
    return h * (1.0 + scale) + shift


def dwconv3(x, w):
    ch = x.shape[-1]
    return lax.conv_general_dilated(
        x, w[:, None, :].astype(x.dtype), window_strides=(1,), padding=[(CONV_W // 2, CONV_W // 2)],
        dimension_numbers=("NWC", "WIO", "NWC"), feature_group_count=ch)


def axial_rope_tables(n):
    rows = n // GRID_W
    row = jnp.broadcast_to(jnp.arange(rows, dtype=jnp.float32)[:, None], (rows, GRID_W)).reshape(n)
    col = jnp.broadcast_to(jnp.arange(GRID_W, dtype=jnp.float32)[None, :], (rows, GRID_W)).reshape(n)
    half = HEAD_DIM // 2
    inv = ROPE_THETA ** (-jnp.arange(0, half, 2, dtype=jnp.float32) / half)
    ang = jnp.concatenate([row[:, None] * inv, col[:, None] * inv], axis=-1)
    return jnp.cos(ang), jnp.sin(ang)


def apply_rope(t, cos, sin):
    half = HEAD_DIM // 2
    tf = t.astype(jnp.float32)
    t1, t2 = tf[..., :half], tf[..., half:]
    cs, sn = cos[None, :, None, :], sin[None, :, None, :]
    return jnp.concatenate([t1 * cs - t2 * sn, t1 * sn + t2 * cs], axis=-1).astype(t.dtype)


def latent_attention(q, k, v, kc, vc, sink):
    b, n = q.shape[:2]
    nb = n // BLK
    grp = N_HEADS // N_KV_HEADS
    m = kc.shape[1]
    scale = HEAD_DIM ** -0.5
    qb = q.reshape(b, nb, BLK, N_KV_HEADS, grp, HEAD_DIM)

    def neighbours(t):
        tb = t.reshape(b, nb, BLK, N_KV_HEADS, HEAD_DIM)
        tp = jnp.pad(tb, ((0, 0), (1, 1), (0, 0), (0, 0), (0, 0)))
        return jnp.concatenate([tp[:, :-2], tp[:, 1:-1], tp[:, 2:]], axis=2)

    kn, vn = neighbours(k), neighbours(v)
    rel = (jnp.arange(3 * BLK)[None, :] - BLK) - jnp.arange(BLK)[:, None]
    band = jnp.abs(rel) <= WINDOW
    src_blk = jnp.arange(nb)[:, None] + jnp.arange(3 * BLK)[None, :] // BLK - 1
    in_range = (src_blk >= 0) & (src_blk < nb)
    sink_l = sink.astype(jnp.float32).reshape(N_KV_HEADS, grp)[None, :, :, None, None]

    def block(args):
        qi, ki, vi, ok = args
        s_loc = jnp.einsum("bqkgd,bjkd->bkgqj", qi, ki).astype(jnp.float32) * scale
        s_loc = jnp.where(band & ok[None, :], s_loc, NEG)
        s_ctx = jnp.einsum("bqkgd,bckd->bkgqc", qi, kc).astype(jnp.float32) * scale
        s_snk = jnp.broadcast_to(sink_l, s_ctx.shape[:-1] + (1,))
        p = jax.nn.softmax(jnp.concatenate([s_loc, s_ctx, s_snk], axis=-1), axis=-1).astype(vi.dtype)
        return (jnp.einsum("bkgqj,bjkd->bqkgd", p[..., :3 * BLK], vi)
                + jnp.einsum("bkgqc,bckd->bqkgd", p[..., 3 * BLK:3 * BLK + m], vc))

    xs = (jnp.moveaxis(qb, 1, 0), jnp.moveaxis(kn, 1, 0), jnp.moveaxis(vn, 1, 0), in_range)
    out = lax.map(block, xs)
    return jnp.moveaxis(out, 0, 1).reshape(b, n, Q_W)


def context_attention(qc, kc, vc, sink):
    b, m = qc.shape[:2]
    grp = N_HEADS // N_KV_HEADS
    qg = qc.reshape(b, m, N_KV_HEADS, grp, HEAD_DIM)
    s = jnp.einsum("bqkgd,bckd->bkgqc", qg, kc).astype(jnp.float32) * (HEAD_DIM ** -0.5)
    s_snk = jnp.broadcast_to(sink.astype(jnp.float32).reshape(N_KV_HEADS, grp)[None, :, :, None, None],
                             s.shape[:-1] + (1,))
    p = jax.nn.softmax(jnp.concatenate([s, s_snk], axis=-1), axis=-1).astype(vc.dtype)
    o = jnp.einsum("bkgqc,bckd->bqkgd", p[..., :m], vc)
    return o.reshape(b, m, Q_W)


def chunk_spatial_gating(z_a, w_s, b_s, g_v):
    z = jax.nn.gelu(z_a)
    u, v = z[..., :A_W], z[..., A_W:]
    v = rmsnorm(v, g_v)
    b, n = v.shape[:2]
    vr = v.reshape(b, n // CHUNK, CHUNK, A_GROUPS, A_GW)
    mixed = jnp.einsum("gpq,bnqgc->bnpgc", w_s, vr) + b_s.T[None, None, :, :, None]
    return u * mixed.reshape(b, n, A_W)


def short_conv_mixer(z_b, w_sconv):
    bg, cg, hb = z_b[..., :B_W], z_b[..., B_W:2 * B_W], z_b[..., 2 * B_W:]
    return bg * dwconv3(cg * hb, w_sconv)


def merge_branches(z, y_attn, w_s, b_s, g_v, w_sconv, b_gate, w_branch, w_out):
    y_a = chunk_spatial_gating(z[..., OFF_A:OFF_B], w_s, b_s, g_v)
    y_b = short_conv_mixer(z[..., OFF_B:OFF_G], w_sconv)
    gates = jax.nn.sigmoid(z[..., OFF_G:] + b_gate)
    g_c, g_a, g_b = gates[..., :D_MODEL], gates[..., D_MODEL:2 * D_MODEL], gates[..., 2 * D_MODEL:]
    merged = (g_c * (y_attn @ w_branch[0]) + g_a * (y_a @ w_branch[1]) + g_b * (y_b @ w_branch[2]))
    return merged @ w_out


def conv_ffn(h, w_up, w_fconv, w_down):
    up = h @ w_up
    a, g = up[..., :D_FF], up[..., D_FF:]
    return (jax.nn.silu(dwconv3(a, w_fconv)) * g) @ w_down


def setup_inputs(seed: int = 0) -> dict:
    key = jax.random.key(seed)
    ks = jax.random.split(key, 24)
    f32 = jnp.float32

    def nrm(k, shape, scale):
        return jax.random.normal(k, shape, f32) * scale

    return {
        "x": nrm(ks[0], (BATCH, SEQ, D_MODEL), 1.0),
        "c": nrm(ks[1], (BATCH, D_MODEL), 1.0),
        "ctx": nrm(ks[2], (BATCH, CTX_LEN, D_MODEL), 1.0),
        "c_ctx": nrm(ks[3], (D_MODEL,), 1.0),
        "w_mod": nrm(ks[4], (DEPTH, D_MODEL, 6 * D_MODEL), D_MODEL ** -0.5),
        "b_mod": nrm(ks[5], (DEPTH, 6 * D_MODEL), 0.02),
        "g_mix": 1.0 + nrm(ks[6], (DEPTH, D_MODEL), 0.02),
        "w_in": nrm(ks[7], (DEPTH, D_MODEL, IN_W), D_MODEL ** -0.5),
        "b_gate": nrm(ks[8], (DEPTH, N_BRANCH * D_MODEL), 0.02),
        "sink": nrm(ks[9], (DEPTH, N_HEADS), 0.5),
        "w_spatial": nrm(ks[10], (DEPTH, A_GROUPS, CHUNK, CHUNK), CHUNK ** -0.5),
        "b_spatial": 1.0 + nrm(ks[11], (DEPTH, A_GROUPS, CHUNK), 0.02),
        "g_v": 1.0 + nrm(ks[12], (DEPTH, A_W), 0.02),
        "w_sconv": nrm(ks[13], (DEPTH, CONV_W, B_W), CONV_W ** -0.5),
        "w_branch": nrm(ks[14], (DEPTH, N_BRANCH, BRANCH_W, D_MODEL), BRANCH_W ** -0.5),
        "w_out": nrm(ks[15], (DEPTH, D_MODEL, D_MODEL), D_MODEL ** -0.5),
        "g_ffn": 1.0 + nrm(ks[16], (DEPTH, D_MODEL), 0.02),
        "w_up": nrm(ks[17], (DEPTH, D_MODEL, 2 * D_FF), D_MODEL ** -0.5),
        "w_fconv": nrm(ks[18], (DEPTH, CONV_W, D_FF), CONV_W ** -0.5),
        "w_down": nrm(ks[19], (DEPTH, D_FF, D_MODEL), D_FF ** -0.5),
        "g_final": 1.0 + nrm(ks[20], (D_MODEL,), 0.02),
    }


def reference(x, c, ctx, c_ctx, w_mod, b_mod, g_mix, w_in, b_gate, sink, w_spatial, b_spatial, g_v,
              w_sconv, w_branch, w_out, g_ffn, w_up, w_fconv, w_down, g_final):
    b, n, _ = x.shape
    m = ctx.shape[1]
    cos, sin = axial_rope_tables(n)
    xc = ctx
    for l in range(DEPTH):
        last = l == DEPTH - 1
        mod = jax.nn.silu(c) @ w_mod[l] + b_mod[l]
        mod_c = jax.nn.silu(c_ctx) @ w_mod[l] + b_mod[l]
        sh1, sc1, gt1, sh2, sc2, gt2 = [t[:, None, :] for t in jnp.split(mod, 6, axis=-1)]
        csh1, csc1, cgt1, csh2, csc2, cgt2 = jnp.split(mod_c, 6, axis=-1)

        h = modulate(rmsnorm(x, g_mix[l]), sh1, sc1)
        hc = modulate(rmsnorm(xc, g_mix[l]), csh1, csc1)
        z = h @ w_in[l]
        q = apply_rope(z[..., OFF_Q:OFF_K].reshape(b, n, N_HEADS, HEAD_DIM), cos, sin)
        k = apply_rope(z[..., OFF_K:OFF_V].reshape(b, n, N_KV_HEADS, HEAD_DIM), cos, sin)
        v = z[..., OFF_V:OFF_A].reshape(b, n, N_KV_HEADS, HEAD_DIM)
        if last:
            zc = hc @ w_in[l][:, OFF_K:OFF_A]
            kc = zc[..., :KV_W].reshape(b, m, N_KV_HEADS, HEAD_DIM)
            vc = zc[..., KV_W:].reshape(b, m, N_KV_HEADS, HEAD_DIM)
        else:
            zc = hc @ w_in[l]
            kc = zc[..., OFF_K:OFF_V].reshape(b, m, N_KV_HEADS, HEAD_DIM)
            vc = zc[..., OFF_V:OFF_A].reshape(b, m, N_KV_HEADS, HEAD_DIM)
        y_attn = latent_attention(q, k, v, kc, vc, sink[l])
        x = x + gt1 * merge_branches(z, y_attn, w_spatial[l], b_spatial[l], g_v[l], w_sconv[l],
                                     b_gate[l], w_branch[l], w_out[l])
        h2 = modulate(rmsnorm(x, g_ffn[l]), sh2, sc2)
        x = x + gt2 * conv_ffn(h2, w_up[l], w_fconv[l], w_down[l])

        if not last:
            qc = zc[..., OFF_Q:OFF_K].reshape(b, m, N_HEADS, HEAD_DIM)
            yc_attn = context_attention(qc, kc, vc, sink[l])
            xc = xc + cgt1 * merge_branches(zc, yc_attn, w_spatial[l], b_spatial[l], g_v[l], w_sconv[l],
                                            b_gate[l], w_branch[l], w_out[l])
            hc2 = modulate(rmsnorm(xc, g_ffn[l]), csh2, csc2)
            xc = xc + cgt2 * conv_ffn(hc2, w_up[l], w_fconv[l], w_down[l])
    return rmsnorm(x, g_final)
```

```python
import functools

import jax
import jax.numpy as jnp
from jax import lax
from jax.experimental import pallas as pl
from jax.experimental.pallas import tpu as pltpu

F32 = jnp.float32
BF16 = jnp.bfloat16

D_MODEL = 1024
DEPTH = 2
GRID_W = 64
N_HEADS = 16
N_KV_HEADS = 4
HEAD_DIM = 64
Q_W = N_HEADS * HEAD_DIM
KV_W = N_KV_HEADS * HEAD_DIM
BLK = 128
ROPE_THETA = 10000.0
CHUNK = 128
A_W = D_MODEL
A_GROUPS = 8
A_GW = A_W // A_GROUPS
B_W = D_MODEL
N_BRANCH = 3
D_FF = 2816
EPS = 1e-6
NEG = -1e30
OFF_Q = 0
OFF_K = OFF_Q + Q_W
OFF_V = OFF_K + KV_W
OFF_A = OFF_V + KV_W
OFF_B = OFF_A + 2 * A_W
OFF_G = OFF_B + 3 * B_W
IN_W = OFF_G + N_BRANCH * D_MODEL
QKV_W = OFF_A

LANES = 128
HALO = 16
VMEM_LIMIT = 56 * 1024 * 1024
MOD_ROWS = 8
CTX_ROW = 4


def _cparams(n_axes):
    return pltpu.CompilerParams(
        dimension_semantics=("arbitrary",) * n_axes, vmem_limit_bytes=VMEM_LIMIT)


def _rms(x):
    return x * lax.rsqrt(jnp.mean(x * x, axis=-1, keepdims=True) + EPS)


def _mod_kernel(c_ref, w_ref, b_ref, o_ref):
    a = jax.nn.silu(c_ref[...]).astype(BF16)
    o_ref[...] = jnp.dot(a, w_ref[...].astype(BF16), preferred_element_type=F32) + b_ref[...]


def _modulation(c_all, w_mod, b_mod):
    tn = 1536
    n_out = 6 * D_MODEL
    return pl.pallas_call(
        _mod_kernel,
        grid=(DEPTH, n_out // tn),
        in_specs=[
            pl.BlockSpec((MOD_ROWS, D_MODEL), lambda l, j: (0, 0)),
            pl.BlockSpec((None, D_MODEL, tn), lambda l, j: (l, 0, j)),
            pl.BlockSpec((None, 1, tn), lambda l, j: (l, 0, j)),
        ],
        out_specs=pl.BlockSpec((None, MOD_ROWS, tn), lambda l, j: (l, 0, j)),
        out_shape=jax.ShapeDtypeStruct((DEPTH, MOD_ROWS, n_out), F32),
        compiler_params=_cparams(2),
        name="modulation",
    )(c_all, w_mod, b_mod.reshape(DEPTH, 1, n_out))


def _inproj_kernel(*refs, sections, rope, tn):
    if rope:
        x_ref, mod_ref, g_ref, cos_ref, sa_ref, sb_ref, w_ref = refs[:7]
        out_refs = refs[7:]
    else:
        x_ref, mod_ref, g_ref, w_ref = refs[:4]
        out_refs = refs[4:]
    h = _rms(x_ref[...]) * g_ref[...]
    h = h * (1.0 + mod_ref[:, D_MODEL:2 * D_MODEL]) + mod_ref[:, 0:D_MODEL]
    hb = h.astype(BF16)
    for (start, width, rope_w), o_ref in zip(sections, out_refs):
        for c0 in range(0, width, tn):
            w = min(tn, width - c0)
            acc = jnp.dot(hb, w_ref[:, start + c0:start + c0 + w], preferred_element_type=F32)
            for cc in range(0, w, LANES):
                t = acc[:, cc:cc + LANES]
                if rope and c0 + cc < rope_w:
                    t = (t * cos_ref[...] + pltpu.roll(t, 32, 1) * sa_ref[...]
                         + pltpu.roll(t, LANES - 32, 1) * sb_ref[...])
                o_ref[:, c0 + cc:c0 + cc + LANES] = t.astype(BF16)


def _inproj(x, mod, mod_row, g, w, sections, tm, rope_tabs=None):
    b, s, _ = x.shape
    nt = s // tm
    rope = rope_tabs is not None
    in_specs = [
        pl.BlockSpec((None, tm, D_MODEL), lambda bi, i: (bi, i, 0)),
        pl.BlockSpec((None, 1, 6 * D_MODEL), lambda bi, i: (mod_row(bi), 0, 0)),
        pl.BlockSpec((1, D_MODEL), lambda bi, i: (0, 0)),
    ]
    args = [x, mod, g]
    if rope:
        in_specs += [pl.BlockSpec((tm, LANES), lambda bi, i: (i, 0))] * 3
        args += list(rope_tabs)
    in_specs.append(pl.BlockSpec(w.shape, lambda bi, i: (0, 0)))
    args.append(w)
    outs = pl.pallas_call(
        functools.partial(_inproj_kernel, sections=tuple(sections), rope=rope, tn=512),
        grid=(b, nt),
        in_specs=in_specs,
        out_specs=[pl.BlockSpec((None, tm, wd), lambda bi, i: (bi, i, 0)) for (_, wd, _) in sections],
        out_shape=[jax.ShapeDtypeStruct((b, s, wd), BF16) for (_, wd, _) in sections],
        compiler_params=_cparams(2),
        name="inproj",
    )(*args)
    return outs


def _dup_head(ref, kh):
    x = ref[:, (kh // 2) * LANES:(kh // 2 + 1) * LANES].astype(F32)
    r = pltpu.roll(x, HEAD_DIM, 1)
    lo = lax.broadcasted_iota(jnp.int32, x.shape, 1) < HEAD_DIM
    out = jnp.where(lo, x, r) if kh % 2 == 0 else jnp.where(lo, r, x)
    return out.astype(BF16)


def _attn_kernel(*refs, local, nb):
    if local:
        q_ref, kp_ref, km_ref, kn_ref, vp_ref, vm_ref, vn_ref, kc_ref, vc_ref, sink_ref, y_ref = refs
    else:
        q_ref, kc_ref, vc_ref, sink_ref, y_ref = refs
    tq = q_ref.shape[0]
    grp = N_HEADS // N_KV_HEADS
    scale = HEAD_DIM ** -0.5
    i = pl.program_id(1)
    lo_q = lax.broadcasted_iota(jnp.int32, (tq, LANES), 1) < HEAD_DIM
    if local:
        qi = lax.broadcasted_iota(jnp.int32, (grp * tq, BLK), 0) & (BLK - 1)
        kj = lax.broadcasted_iota(jnp.int32, (grp * tq, BLK), 1)
        mask_prev = kj >= qi + jnp.where(i > 0, 0, BLK)
        mask_next = kj <= qi - jnp.where(i < nb - 1, 0, BLK)
    dn = (((1,), (1,)), ((), ()))
    for kh in range(N_KV_HEADS):
        qs = []
        for p in range(2 * kh, 2 * kh + 2):
            qp = q_ref[:, p * LANES:(p + 1) * LANES]
            zero = jnp.zeros_like(qp)
            qs += [jnp.where(lo_q, qp, zero), jnp.where(lo_q, zero, qp)]
        qm = jnp.concatenate(qs, axis=0)
        sink = jnp.concatenate(
            [jnp.full((tq, 1), sink_ref[kh * grp + r], F32) for r in range(grp)], axis=0)
        k_c, v_c = _dup_head(kc_ref, kh), _dup_head(vc_ref, kh)
        s_c = lax.dot_general(qm, k_c, dn, preferred_element_type=F32) * scale
        m = jnp.maximum(jnp.max(s_c, axis=-1, keepdims=True), sink)
        if local:
            ks = [_dup_head(r, kh) for r in (kp_ref, km_ref, kn_ref)]
            vs = [_dup_head(r, kh) for r in (vp_ref, vm_ref, vn_ref)]
            ss = [lax.dot_general(qm, kk, dn, preferred_element_type=F32) * scale for kk in ks]
            ss[0] = jnp.where(mask_prev, ss[0], NEG)
            ss[2] = jnp.where(mask_next, ss[2], NEG)
            for s_l in ss:
                m = jnp.maximum(m, jnp.max(s_l, axis=-1, keepdims=True))
        p_c = jnp.exp(s_c - m)
        den = jnp.sum(p_c, axis=-1, keepdims=True) + jnp.exp(sink - m)
        o = jnp.dot(p_c.astype(BF16), v_c, preferred_element_type=F32)
        if local:
            for s_l, vv in zip(ss, vs):
                p_l = jnp.exp(s_l - m)
                den = den + jnp.sum(p_l, axis=-1, keepdims=True)
                o = o + jnp.dot(p_l.astype(BF16), vv, preferred_element_type=F32)
        o = o / den
        for pi in range(2):
            y = jnp.where(lo_q, o[(2 * pi) * tq:(2 * pi + 1) * tq], o[(2 * pi + 1) * tq:(2 * pi + 2) * tq])
            p = 2 * kh + pi
            y_ref[:, p * LANES:(p + 1) * LANES] = y.astype(BF16)


def _attention(zq, q_col, zc, kc_col, vc_col, sink, local, zkv=None, k_col=None, v_col=None):
    b, s = zq.shape[:2]
    m = zc.shape[1]
    tq = BLK if local else m
    nb = s // tq
    in_specs = [pl.BlockSpec((None, tq, Q_W), lambda bi, i: (bi, i, q_col))]
    args = [zq]
    if local:
        for col in (k_col, v_col):
            in_specs += [
                pl.BlockSpec((None, BLK, KV_W), lambda bi, i, col=col: (bi, jnp.maximum(i - 1, 0), col)),
                pl.BlockSpec((None, BLK, KV_W), lambda bi, i, col=col: (bi, i, col)),
                pl.BlockSpec((None, BLK, KV_W), lambda bi, i, col=col: (bi, jnp.minimum(i + 1, nb - 1), col)),
            ]
            args += [zkv, zkv, zkv]
    in_specs += [
        pl.BlockSpec((None, m, KV_W), lambda bi, i: (bi, 0, kc_col)),
        pl.BlockSpec((None, m, KV_W), lambda bi, i: (bi, 0, vc_col)),
        pl.BlockSpec(memory_space=pltpu.SMEM),
    ]
    args += [zc, zc, sink]
    return pl.pallas_call(
        functools.partial(_attn_kernel, local=local, nb=nb),
        grid=(b, nb),
        in_specs=in_specs,
        out_specs=pl.BlockSpec((None, tq, Q_W), lambda bi, i: (bi, i, 0)),
        out_shape=jax.ShapeDtypeStruct((b, s, Q_W), BF16),
        compiler_params=_cparams(2),
        name="attention",
    )(*args)


def _merge_kernel(x_ref, mod_ref, y_ref, za_ref, zb_ref, zbp_ref, zbn_ref, zg_ref, ws_ref, bs_ref,
                  gv_ref, wc_ref, bg_ref, wb_ref, wo_ref, o_ref, ya_ref, *, nt):
    tm = x_ref.shape[0]
    i = pl.program_id(1)
    for n in range(tm // CHUNK):
        rows = slice(n * CHUNK, (n + 1) * CHUNK)
        z = jax.nn.gelu(za_ref[rows, :].astype(F32))
        u, v = z[:, :A_W], z[:, A_W:]
        vn = (_rms(v) * gv_ref[...]).astype(BF16)
        for g in range(A_GROUPS):
            cols = slice(g * A_GW, (g + 1) * A_GW)
            mixed = jnp.dot(ws_ref[g], vn[:, cols], preferred_element_type=F32) + bs_ref[:, cols]
            ya_ref[rows, cols] = (u[:, cols] * mixed).astype(BF16)
    zb = zb_ref[...].astype(F32)
    bgate, prod = zb[:, :B_W], zb[:, B_W:2 * B_W] * zb[:, 2 * B_W:]
    zp = zbp_ref[HALO - 1:HALO, :].astype(F32)
    zn = zbn_ref[0:1, :].astype(F32)
    prev_row = jnp.where(i > 0, zp[:, B_W:2 * B_W] * zp[:, 2 * B_W:], 0.0)
    next_row = jnp.where(i < nt - 1, zn[:, B_W:2 * B_W] * zn[:, 2 * B_W:], 0.0)
    r = lax.broadcasted_iota(jnp.int32, (tm, B_W), 0)
    pm1 = jnp.where(r == 0, prev_row, pltpu.roll(prod, 1, 0))
    pp1 = jnp.where(r == tm - 1, next_row, pltpu.roll(prod, tm - 1, 0))
    y_b = bgate * (wc_ref[0:1, :] * pm1 + wc_ref[1:2, :] * prod + wc_ref[2:3, :] * pp1)
    gates = jax.nn.sigmoid(zg_ref[...].astype(F32) + bg_ref[...])
    merged = (gates[:, :D_MODEL] * jnp.dot(y_ref[...], wb_ref[0], preferred_element_type=F32)
              + gates[:, D_MODEL:2 * D_MODEL] * jnp.dot(ya_ref[...], wb_ref[1], preferred_element_type=F32)
              + gates[:, 2 * D_MODEL:] * jnp.dot(y_b.astype(BF16), wb_ref[2], preferred_element_type=F32))
    out = jnp.dot(merged.astype(BF16), wo_ref[...], preferred_element_type=F32)
    o_ref[...] = x_ref[...] + mod_ref[:, 2 * D_MODEL:3 * D_MODEL] * out


def _merge(x, mod, mod_row, y, za, zb, zg, ws, bs, gv, wc, bg, wb, wo, tm):
    b, s, _ = x.shape
    nt = s // tm
    hb = tm // HALO
    nh = s // HALO
    full = lambda a: pl.BlockSpec(a.shape, lambda bi, i: (0,) * a.ndim)
    in_specs = [
        pl.BlockSpec((None, tm, D_MODEL), lambda bi, i: (bi, i, 0)),
        pl.BlockSpec((None, 1, 6 * D_MODEL), lambda bi, i: (mod_row(bi), 0, 0)),
        pl.BlockSpec((None, tm, Q_W), lambda bi, i: (bi, i, 0)),
        pl.BlockSpec((None, tm, 2 * A_W), lambda bi, i: (bi, i, 0)),
        pl.BlockSpec((None, tm, 3 * B_W), lambda bi, i: (bi, i, 0)),
        pl.BlockSpec((None, HALO, 3 * B_W), lambda bi, i: (bi, jnp.maximum(i * hb - 1, 0), 0)),
        pl.BlockSpec((None, HALO, 3 * B_W), lambda bi, i: (bi, jnp.minimum((i + 1) * hb, nh - 1), 0)),
        pl.BlockSpec((None, tm, N_BRANCH * D_MODEL), lambda bi, i: (bi, i, 0)),
        full(ws), full(bs), full(gv), full(wc), full(bg), full(wb), full(wo),
    ]
    return pl.pallas_call(
        functools.partial(_merge_kernel, nt=nt),
        grid=(b, nt),
        in_specs=in_specs,
        out_specs=pl.BlockSpec((None, tm, D_MODEL), lambda bi, i: (bi, i, 0)),
        out_shape=jax.ShapeDtypeStruct((b, s, D_MODEL), F32),
        scratch_shapes=[pltpu.VMEM((tm, A_W), BF16)],
        compiler_params=_cparams(2),
        name="merge",
    )(x, mod, y, za, zb, zb, zb, zg, ws, bs, gv, wc, bg, wb, wo)


def _ffn_kernel(x_ref, xp_ref, xn_ref, mod_ref, g_ref, wu_ref, wc_ref, wd_ref, gf_ref, o_ref, act_ref,
                *, nt, final, fc):
    tm = x_ref.shape[0]
    i = pl.program_id(1)
    sh, sc = mod_ref[:, 3 * D_MODEL:4 * D_MODEL], mod_ref[:, 4 * D_MODEL:5 * D_MODEL]

    def hmod(x):
        return ((_rms(x) * g_ref[...]) * (1.0 + sc) + sh).astype(BF16)

    x = x_ref[...]
    hm = hmod(x)
    he = jnp.concatenate([hmod(xp_ref[...]), hm, hmod(xn_ref[...])], axis=0)
    r = lax.broadcasted_iota(jnp.int32, (tm, fc), 0)
    first = r < jnp.where(i == 0, 1, 0)
    last = r >= tm - jnp.where(i == nt - 1, 1, 0)
    for c in range(0, D_FF, fc):
        a = jnp.dot(he, wu_ref[:, c:c + fc], preferred_element_type=F32)
        gt = jnp.dot(hm, wu_ref[:, D_FF + c:D_FF + c + fc], preferred_element_type=F32)
        am1 = jnp.where(first, 0.0, pltpu.roll(a, 1, 0)[HALO:HALO + tm])
        ap1 = jnp.where(last, 0.0, pltpu.roll(a, tm + 2 * HALO - 1, 0)[HALO:HALO + tm])
        conv = (wc_ref[0:1, c:c + fc] * am1 + wc_ref[1:2, c:c + fc] * a[HALO:HALO + tm]
                + wc_ref[2:3, c:c + fc] * ap1)
        act_ref[:, c:c + fc] = (jax.nn.silu(conv) * gt).astype(BF16)
    y = jnp.dot(act_ref[...], wd_ref[...], preferred_element_type=F32)
    out = x + mod_ref[:, 5 * D_MODEL:6 * D_MODEL] * y
    if final:
        out = _rms(out) * gf_ref[...]
    o_ref[...] = out


def _ffn(x, mod, mod_row, g, wu, wc, wd, gf, tm, final):
    b, s, _ = x.shape
    nt = s // tm
    hb = tm // HALO
    nh = s // HALO
    full = lambda a: pl.BlockSpec(a.shape, lambda bi, i: (0,) * a.ndim)
    in_specs = [
        pl.BlockSpec((None, tm, D_MODEL), lambda bi, i: (bi, i, 0)),
        pl.BlockSpec((None, HALO, D_MODEL), lambda bi, i: (bi, jnp.maximum(i * hb - 1, 0), 0)),
        pl.BlockSpec((None, HALO, D_MODEL), lambda bi, i: (bi, jnp.minimum((i + 1) * hb, nh - 1), 0)),
        pl.BlockSpec((None, 1, 6 * D_MODEL), lambda bi, i: (mod_row(bi), 0, 0)),
        full(g), full(wu), full(wc), full(wd), full(gf),
    ]
    return pl.pallas_call(
        functools.partial(_ffn_kernel, nt=nt, final=final, fc=256),
        grid=(b, nt),
        in_specs=in_specs,
        out_specs=pl.BlockSpec((None, tm, D_MODEL), lambda bi, i: (bi, i, 0)),
        out_shape=jax.ShapeDtypeStruct((b, s, D_MODEL), F32),
        scratch_shapes=[pltpu.VMEM((tm, D_FF), BF16)],
        compiler_params=_cparams(2),
        name="ffn",
    )(x, x, x, mod, g, wu, wc, wd, gf)


def _rope_tables(n):
    rows = n // GRID_W
    row = jnp.broadcast_to(jnp.arange(rows, dtype=F32)[:, None], (rows, GRID_W)).reshape(n)
    col = jnp.broadcast_to(jnp.arange(GRID_W, dtype=F32)[None, :], (rows, GRID_W)).reshape(n)
    half = HEAD_DIM // 2
    inv = ROPE_THETA ** (-jnp.arange(0, half, 2, dtype=F32) / half)
    ang = jnp.concatenate([row[:, None] * inv, col[:, None] * inv], axis=-1)
    cos, sin = jnp.cos(ang), jnp.sin(ang)
    zero = jnp.zeros_like(sin)
    reps = LANES // HEAD_DIM
    cos_t = jnp.tile(jnp.concatenate([cos, cos], axis=-1), (1, reps))
    sin_hi = jnp.tile(jnp.concatenate([zero, sin], axis=-1), (1, reps))
    sin_lo = jnp.tile(jnp.concatenate([-sin, zero], axis=-1), (1, reps))
    return cos_t, sin_hi, sin_lo


def kernel(x, c, ctx, c_ctx, w_mod, b_mod, g_mix, w_in, b_gate, sink, w_spatial, b_spatial, g_v, w_sconv,
           w_branch, w_out, g_ffn, w_up, w_fconv, w_down, g_final):
    b, n, _ = x.shape
    m = ctx.shape[1]
    assert b + 1 <= MOD_ROWS and n % 512 == 0 and m % CHUNK == 0

    c_all = jnp.zeros((MOD_ROWS, D_MODEL), F32).at[:b].set(c).at[CTX_ROW].set(c_ctx)
    mod = _modulation(c_all, w_mod, b_mod).reshape(DEPTH, MOD_ROWS, 1, 6 * D_MODEL)
    rope_tabs = _rope_tables(n)

    w_in_b = w_in.astype(BF16)
    w_branch_b = w_branch.astype(BF16)
    w_out_b = w_out.astype(BF16)
    w_up_b = w_up.astype(BF16)
    w_down_b = w_down.astype(BF16)
    w_sp_b = w_spatial.astype(BF16)
    bs_full = jnp.repeat(jnp.transpose(b_spatial, (0, 2, 1)), A_GW, axis=2)

    lat_row = lambda bi: bi
    ctx_row = lambda bi: CTX_ROW
    full_sections = ((OFF_Q, QKV_W, OFF_V), (OFF_A, 2 * A_W, 0), (OFF_B, 3 * B_W, 0), (OFF_G, N_BRANCH * D_MODEL, 0))
    ctx_sections = tuple((s0, wd, 0) for (s0, wd, _) in full_sections)
    k_blk, v_blk = OFF_K // KV_W, OFF_V // KV_W

    xc = ctx
    for l in range(DEPTH):
        last = l == DEPTH - 1
        mod_l = mod[l]
        g_mix_l = g_mix[l].reshape(1, D_MODEL)
        merge_w = (w_sp_b[l], bs_full[l], g_v[l].reshape(1, A_W), w_sconv[l], b_gate[l].reshape(1, -1),
                   w_branch_b[l], w_out_b[l])
        ffn_w = (g_ffn[l].reshape(1, D_MODEL), w_up_b[l], w_fconv[l], w_down_b[l], g_final.reshape(1, D_MODEL))

        zqkv, za, zb, zg = _inproj(x, mod_l, lat_row, g_mix_l, w_in_b[l], full_sections, 256, rope_tabs)
        if last:
            (zc_kv,) = _inproj(xc, mod_l, ctx_row, g_mix_l, w_in_b[l][:, OFF_K:OFF_A], ((0, 2 * KV_W, 0),), m)
            zc, kc_blk, vc_blk = zc_kv, 0, 1
        else:
            zc_qkv, zc_a, zc_b, zc_g = _inproj(xc, mod_l, ctx_row, g_mix_l, w_in_b[l], ctx_sections, m)
            zc, kc_blk, vc_blk = zc_qkv, k_blk, v_blk
        y = _attention(zqkv, 0, zc, kc_blk, vc_blk, sink[l], True, zkv=zqkv, k_col=k_blk, v_col=v_blk)
        x = _merge(x, mod_l, lat_row, y, za, zb, zg, *merge_w, tm=256)
        x = _ffn(x, mod_l, lat_row, *ffn_w, tm=512, final=last)
        if not last:
            yc = _attention(zc_qkv, 0, zc_qkv, k_blk, v_blk, sink[l], False)
            xc = _merge(xc, mod_l, ctx_row, yc, zc_a, zc_b, zc_g, *merge_w, tm=m)
            xc = _ffn(xc, mod_l, ctx_row, *ffn_w, tm=m, final=False)
    return x
```

```python
import functools

import jax
import jax.numpy as jnp
from jax import lax
from jax.experimental import pallas as pl
from jax.experimental.pallas import tpu as pltpu

F32 = jnp.float32
BF16 = jnp.bfloat16

D_MODEL = 1024
DEPTH = 2
GRID_W = 64
N_HEADS = 16
N_KV_HEADS = 4
HEAD_DIM = 64
Q_W = N_HEADS * HEAD_DIM
KV_W = N_KV_HEADS * HEAD_DIM
BLK = 128
ROPE_THETA = 10000.0
CHUNK = 128
A_W = D_MODEL
A_GROUPS = 8
A_GW = A_W // A_GROUPS
B_W = D_MODEL
N_BRANCH = 3
D_FF = 2816
EPS = 1e-6
NEG = -1e30
OFF_Q = 0
OFF_K = OFF_Q + Q_W
OFF_V = OFF_K + KV_W
OFF_A = OFF_V + KV_W
OFF_B = OFF_A + 2 * A_W
OFF_G = OFF_B + 3 * B_W
IN_W = OFF_G + N_BRANCH * D_MODEL
QKV_W = OFF_A

LOG2E = 1.4426950408889634
QK_SCALE = HEAD_DIM ** -0.5 * LOG2E

LANES = 128
HALO = 16
VMEM_LIMIT = 56 * 1024 * 1024
MOD_ROWS = 8
CTX_ROW = 4


def _cparams(n_axes):
    return pltpu.CompilerParams(
        dimension_semantics=("arbitrary",) * n_axes, vmem_limit_bytes=VMEM_LIMIT)


def _rms(x):
    return x * lax.rsqrt(jnp.mean(x * x, axis=-1, keepdims=True) + EPS)


def _mod_kernel(c_ref, w_ref, b_ref, o_ref):
    a = jax.nn.silu(c_ref[...]).astype(BF16)
    o_ref[...] = jnp.dot(a, w_ref[...].astype(BF16), preferred_element_type=F32) + b_ref[...]


def _modulation(c_all, w_mod, b_mod):
    tn = 1536
    n_out = 6 * D_MODEL
    return pl.pallas_call(
        _mod_kernel,
        grid=(DEPTH, n_out // tn),
        in_specs=[
            pl.BlockSpec((MOD_ROWS, D_MODEL), lambda l, j: (0, 0)),
            pl.BlockSpec((None, D_MODEL, tn), lambda l, j: (l, 0, j)),
            pl.BlockSpec((None, 1, tn), lambda l, j: (l, 0, j)),
        ],
        out_specs=pl.BlockSpec((None, MOD_ROWS, tn), lambda l, j: (l, 0, j)),
        out_shape=jax.ShapeDtypeStruct((DEPTH, MOD_ROWS, n_out), F32),
        compiler_params=_cparams(2),
        name="modulation",
    )(c_all, w_mod, b_mod.reshape(DEPTH, 1, n_out))


def _inproj_kernel(*refs, sections, rope, tn):
    if rope:
        x_ref, mod_ref, g_ref, cos_ref, sa_ref, sb_ref, w_ref = refs[:7]
        out_refs = refs[7:]
    else:
        x_ref, mod_ref, g_ref, w_ref = refs[:4]
        out_refs = refs[4:]
    h = _rms(x_ref[...]) * g_ref[...]
    h = h * (1.0 + mod_ref[:, D_MODEL:2 * D_MODEL]) + mod_ref[:, 0:D_MODEL]
    hb = h.astype(BF16)
    for (start, width, rope_w, scale, transposed), o_ref in zip(sections, out_refs):
        for c0 in range(0, width, tn):
            w = min(tn, width - c0)
            acc = jnp.dot(hb, w_ref[:, start + c0:start + c0 + w], preferred_element_type=F32)
            for cc in range(0, w, LANES):
                t = acc[:, cc:cc + LANES]
                if rope and c0 + cc < rope_w:
                    t = (t * cos_ref[...] + pltpu.roll(t, 32, 1) * sa_ref[...]
                         + pltpu.roll(t, LANES - 32, 1) * sb_ref[...])
                if scale != 1.0:
                    t = t * scale
                if transposed:
                    o_ref[c0 + cc:c0 + cc + LANES, :] = jnp.transpose(t).astype(BF16)
                else:
                    o_ref[:, c0 + cc:c0 + cc + LANES] = t.astype(BF16)


def _inproj(x, mod, mod_row, g, w, sections, tm, rope_tabs=None):
    b, s, _ = x.shape
    nt = s // tm
    rope = rope_tabs is not None
    in_specs = [
        pl.BlockSpec((None, tm, D_MODEL), lambda bi, i: (bi, i, 0)),
        pl.BlockSpec((None, 1, 6 * D_MODEL), lambda bi, i: (mod_row(bi), 0, 0)),
        pl.BlockSpec((1, D_MODEL), lambda bi, i: (0, 0)),
    ]
    args = [x, mod, g]
    if rope:
        in_specs += [pl.BlockSpec((tm, LANES), lambda bi, i: (i, 0))] * 3
        args += list(rope_tabs)
    in_specs.append(pl.BlockSpec(w.shape, lambda bi, i: (0, 0)))
    args.append(w)
    outs = pl.pallas_call(
        functools.partial(_inproj_kernel, sections=tuple(sections), rope=rope, tn=512),
        grid=(b, nt),
        in_specs=in_specs,
        out_specs=[pl.BlockSpec((None, sec[1], tm), lambda bi, i: (bi, 0, i)) if sec[4]
                   else pl.BlockSpec((None, tm, sec[1]), lambda bi, i: (bi, i, 0)) for sec in sections],
        out_shape=[jax.ShapeDtypeStruct((b, sec[1], s) if sec[4] else (b, s, sec[1]), BF16)
                   for sec in sections],
        compiler_params=_cparams(2),
        name="inproj",
    )(*args)
    return outs


def _attn_kernel(*refs, local, nb):
    if local:
        qt_ref, kp_ref, km_ref, kn_ref, vp_ref, vm_ref, vn_ref, kc_ref, vc_ref, sink_ref, yt_ref = refs
    else:
        qt_ref, kc_ref, vc_ref, sink_ref, yt_ref = refs
    nsub = qt_ref.shape[1] // BLK
    grp = N_HEADS // N_KV_HEADS
    width = grp * BLK
    i = pl.program_id(1)
    if local:
        kj = lax.broadcasted_iota(jnp.int32, (BLK, width), 0)
        qi = lax.broadcasted_iota(jnp.int32, (BLK, width), 1) & (BLK - 1)
        band_prev, band_next = kj >= qi, kj <= qi
        edge_prev = kj >= qi + jnp.where(i > 0, 0, BLK)
        edge_next = kj <= qi - jnp.where(i < nb - 1, 0, BLK)
    zeros = jnp.zeros((HEAD_DIM, width), BF16)

    def key_chunks(j, lanes):
        ks = [kc_ref[:, lanes]]
        if local:
            blk = lambda t: slice(t * BLK, (t + 1) * BLK)
            ks.append(kp_ref[:, lanes] if j == 0 else km_ref[blk(j - 1), lanes])
            ks.append(km_ref[blk(j), lanes])
            ks.append(kn_ref[:, lanes] if j == nsub - 1 else km_ref[blk(j + 1), lanes])
        return ks

    def value_chunks(j, rows):
        vs = [vc_ref[rows, :]]
        if local:
            blk = lambda t: slice(t * BLK, (t + 1) * BLK)
            vs.append(vp_ref[rows, :] if j == 0 else vm_ref[rows, blk(j - 1)])
            vs.append(vm_ref[rows, blk(j)])
            vs.append(vn_ref[rows, :] if j == nsub - 1 else vm_ref[rows, blk(j + 1)])
        return vs

    def scores(j, kh):
        pair, half = kh // 2, kh % 2
        cols = slice(j * BLK, (j + 1) * BLK)
        q4 = jnp.concatenate(
            [qt_ref[(kh * grp + r) * HEAD_DIM:(kh * grp + r + 1) * HEAD_DIM, cols] for r in range(grp)], axis=1)
        q_pad = jnp.concatenate([q4, zeros] if half == 0 else [zeros, q4], axis=0)
        ss = [jnp.dot(kk, q_pad, preferred_element_type=F32)
              for kk in key_chunks(j, slice(pair * LANES, (pair + 1) * LANES))]
        if local:
            ss[1] = jnp.where(edge_prev if j == 0 else band_prev, ss[1], NEG)
            ss[3] = jnp.where(edge_next if j == nsub - 1 else band_next, ss[3], NEG)
        return ss

    def finish(j, kh, ss):
        sink = jnp.concatenate(
            [jnp.full((1, BLK), sink_ref[kh * grp + r] * LOG2E, F32) for r in range(grp)], axis=1)
        m = sink
        for s_x in ss:
            m = jnp.maximum(m, jnp.max(s_x, axis=0, keepdims=True))
        den = jnp.exp2(sink - m)
        ps = []
        for s_x in ss:
            p_x = jnp.exp2(s_x - m)
            den = den + jnp.sum(p_x, axis=0, keepdims=True)
            ps.append(p_x.astype(BF16))
        v_t = jnp.concatenate(value_chunks(j, slice(kh * HEAD_DIM, (kh + 1) * HEAD_DIM)), axis=1)
        o_t = jnp.dot(v_t, jnp.concatenate(ps, axis=0), preferred_element_type=F32) / den
        for r in range(grp):
            h = kh * grp + r
            yt_ref[h * HEAD_DIM:(h + 1) * HEAD_DIM, j * BLK:(j + 1) * BLK] = (
                o_t[:, r * BLK:(r + 1) * BLK].astype(BF16))

    items = [(j, kh) for j in range(nsub) for kh in range(N_KV_HEADS)]
    ss_next = scores(*items[0])
    for n, (j, kh) in enumerate(items):
        ss = ss_next
        if n + 1 < len(items):
            ss_next = scores(*items[n + 1])
        finish(j, kh, ss)


def _attention(q_t, kc, vc_t, sink, k=None, v_t=None, tq=512):
    b, _, s = q_t.shape
    m = kc.shape[1]
    local = k is not None
    tq = min(tq, s)
    nt, nsub, nblk = s // tq, tq // BLK, s // BLK
    prev_b = lambda i: jnp.maximum(i * nsub - 1, 0)
    next_b = lambda i: jnp.minimum((i + 1) * nsub, nblk - 1)
    in_specs = [pl.BlockSpec((None, Q_W, tq), lambda bi, i: (bi, 0, i))]
    args = [q_t]
    if local:
        in_specs += [
            pl.BlockSpec((None, BLK, KV_W), lambda bi, i: (bi, prev_b(i), 0)),
            pl.BlockSpec((None, tq, KV_W), lambda bi, i: (bi, i, 0)),
            pl.BlockSpec((None, BLK, KV_W), lambda bi, i: (bi, next_b(i), 0)),
            pl.BlockSpec((None, KV_W, BLK), lambda bi, i: (bi, 0, prev_b(i))),
            pl.BlockSpec((None, KV_W, tq), lambda bi, i: (bi, 0, i)),
            pl.BlockSpec((None, KV_W, BLK), lambda bi, i: (bi, 0, next_b(i))),
        ]
        args += [k, k, k, v_t, v_t, v_t]
    in_specs += [
        pl.BlockSpec((None, m, KV_W), lambda bi, i: (bi, 0, 0)),
        pl.BlockSpec((None, KV_W, m), lambda bi, i: (bi, 0, 0)),
        pl.BlockSpec(memory_space=pltpu.SMEM),
    ]
    args += [kc, vc_t, sink]
    return pl.pallas_call(
        functools.partial(_attn_kernel, local=local, nb=nt),
        grid=(b, nt),
        in_specs=in_specs,
        out_specs=pl.BlockSpec((None, Q_W, tq), lambda bi, i: (bi, 0, i)),
        out_shape=jax.ShapeDtypeStruct((b, Q_W, s), BF16),
        compiler_params=_cparams(2),
        name="attention",
    )(*args)


def _merge_kernel(x_ref, mod_ref, yt_ref, za_ref, zb_ref, zbp_ref, zbn_ref, zg_ref, ws_ref, bs_ref,
                  gv_ref, wc_ref, bg_ref, wb_ref, wo_ref, o_ref, ya_ref, *, nt):
    tm = x_ref.shape[0]
    i = pl.program_id(1)
    for n in range(tm // CHUNK):
        rows = slice(n * CHUNK, (n + 1) * CHUNK)
        z = jax.nn.gelu(za_ref[rows, :].astype(F32))
        u, v = z[:, :A_W], z[:, A_W:]
        vn = (_rms(v) * gv_ref[...]).astype(BF16)
        for g in range(A_GROUPS):
            cols = slice(g * A_GW, (g + 1) * A_GW)
            mixed = jnp.dot(ws_ref[g], vn[:, cols], preferred_element_type=F32) + bs_ref[:, cols]
            ya_ref[rows, cols] = (u[:, cols] * mixed).astype(BF16)
    zb = zb_ref[...].astype(F32)
    bgate, prod = zb[:, :B_W], zb[:, B_W:2 * B_W] * zb[:, 2 * B_W:]
    zp = zbp_ref[HALO - 1:HALO, :].astype(F32)
    zn = zbn_ref[0:1, :].astype(F32)
    prev_row = jnp.where(i > 0, zp[:, B_W:2 * B_W] * zp[:, 2 * B_W:], 0.0)
    next_row = jnp.where(i < nt - 1, zn[:, B_W:2 * B_W] * zn[:, 2 * B_W:], 0.0)
    r = lax.broadcasted_iota(jnp.int32, (tm, B_W), 0)
    pm1 = jnp.where(r == 0, prev_row, pltpu.roll(prod, 1, 0))
    pp1 = jnp.where(r == tm - 1, next_row, pltpu.roll(prod, tm - 1, 0))
    y_b = bgate * (wc_ref[0:1, :] * pm1 + wc_ref[1:2, :] * prod + wc_ref[2:3, :] * pp1)
    gates = jax.nn.sigmoid(zg_ref[...].astype(F32) + bg_ref[...])
    y_attn = lax.dot_general(yt_ref[...], wb_ref[0], (((0,), (0,)), ((), ())), preferred_element_type=F32)
    merged = (gates[:, :D_MODEL] * y_attn
              + gates[:, D_MODEL:2 * D_MODEL] * jnp.dot(ya_ref[...], wb_ref[1], preferred_element_type=F32)
              + gates[:, 2 * D_MODEL:] * jnp.dot(y_b.astype(BF16), wb_ref[2], preferred_element_type=F32))
    out = jnp.dot(merged.astype(BF16), wo_ref[...], preferred_element_type=F32)
    o_ref[...] = x_ref[...] + mod_ref[:, 2 * D_MODEL:3 * D_MODEL] * out


def _merge(x, mod, mod_row, y, za, zb, zg, ws, bs, gv, wc, bg, wb, wo, tm):
    b, s, _ = x.shape
    nt = s // tm
    hb = tm // HALO
    nh = s // HALO
    full = lambda a: pl.BlockSpec(a.shape, lambda bi, i: (0,) * a.ndim)
    in_specs = [
        pl.BlockSpec((None, tm, D_MODEL), lambda bi, i: (bi, i, 0)),
        pl.BlockSpec((None, 1, 6 * D_MODEL), lambda bi, i: (mod_row(bi), 0, 0)),
        pl.BlockSpec((None, Q_W, tm), lambda bi, i: (bi, 0, i)),
        pl.BlockSpec((None, tm, 2 * A_W), lambda bi, i: (bi, i, 0)),
        pl.BlockSpec((None, tm, 3 * B_W), lambda bi, i: (bi, i, 0)),
        pl.BlockSpec((None, HALO, 3 * B_W), lambda bi, i: (bi, jnp.maximum(i * hb - 1, 0), 0)),
        pl.BlockSpec((None, HALO, 3 * B_W), lambda bi, i: (bi, jnp.minimum((i + 1) * hb, nh - 1), 0)),
        pl.BlockSpec((None, tm, N_BRANCH * D_MODEL), lambda bi, i: (bi, i, 0)),
        full(ws), full(bs), full(gv), full(wc), full(bg), full(wb), full(wo),
    ]
    return pl.pallas_call(
        functools.partial(_merge_kernel, nt=nt),
        grid=(b, nt),
        in_specs=in_specs,
        out_specs=pl.BlockSpec((None, tm, D_MODEL), lambda bi, i: (bi, i, 0)),
        out_shape=jax.ShapeDtypeStruct((b, s, D_MODEL), F32),
        scratch_shapes=[pltpu.VMEM((tm, A_W), BF16)],
        compiler_params=_cparams(2),
        name="merge",
    )(x, mod, y, za, zb, zb, zb, zg, ws, bs, gv, wc, bg, wb, wo)


def _ffn_kernel(x_ref, xp_ref, xn_ref, mod_ref, g_ref, wu_ref, wc_ref, wd_ref, gf_ref, o_ref, act_ref,
                *, nt, final, fc):
    tm = x_ref.shape[0]
    i = pl.program_id(1)
    sh, sc = mod_ref[:, 3 * D_MODEL:4 * D_MODEL], mod_ref[:, 4 * D_MODEL:5 * D_MODEL]

    def hmod(x):
        return ((_rms(x) * g_ref[...]) * (1.0 + sc) + sh).astype(BF16)

    x = x_ref[...]
    hm = hmod(x)
    he = jnp.concatenate([hmod(xp_ref[...]), hm, hmod(xn_ref[...])], axis=0)
    r = lax.broadcasted_iota(jnp.int32, (tm, fc), 0)
    first = r < jnp.where(i == 0, 1, 0)
    last = r >= tm - jnp.where(i == nt - 1, 1, 0)
    for c in range(0, D_FF, fc):
        a = jnp.dot(he, wu_ref[:, c:c + fc], preferred_element_type=F32)
        gt = jnp.dot(hm, wu_ref[:, D_FF + c:D_FF + c + fc], preferred_element_type=F32)
        am1 = jnp.where(first, 0.0, pltpu.roll(a, 1, 0)[HALO:HALO + tm])
        ap1 = jnp.where(last, 0.0, pltpu.roll(a, tm + 2 * HALO - 1, 0)[HALO:HALO + tm])
        conv = (wc_ref[0:1, c:c + fc] * am1 + wc_ref[1:2, c:c + fc] * a[HALO:HALO + tm]
                + wc_ref[2:3, c:c + fc] * ap1)
        act_ref[:, c:c + fc] = (jax.nn.silu(conv) * gt).astype(BF16)
    y = jnp.dot(act_ref[...], wd_ref[...], preferred_element_type=F32)
    out = x + mod_ref[:, 5 * D_MODEL:6 * D_MODEL] * y
    if final:
        out = _rms(out) * gf_ref[...]
    o_ref[...] = out


def _ffn(x, mod, mod_row, g, wu, wc, wd, gf, tm, final):
    b, s, _ = x.shape
    nt = s // tm
    hb = tm // HALO
    nh = s // HALO
    full = lambda a: pl.BlockSpec(a.shape, lambda bi, i: (0,) * a.ndim)
    in_specs = [
        pl.BlockSpec((None, tm, D_MODEL), lambda bi, i: (bi, i, 0)),
        pl.BlockSpec((None, HALO, D_MODEL), lambda bi, i: (bi, jnp.maximum(i * hb - 1, 0), 0)),
        pl.BlockSpec((None, HALO, D_MODEL), lambda bi, i: (bi, jnp.minimum((i + 1) * hb, nh - 1), 0)),
        pl.BlockSpec((None, 1, 6 * D_MODEL), lambda bi, i: (mod_row(bi), 0, 0)),
        full(g), full(wu), full(wc), full(wd), full(gf),
    ]
    return pl.pallas_call(
        functools.partial(_ffn_kernel, nt=nt, final=final, fc=256),
        grid=(b, nt),
        in_specs=in_specs,
        out_specs=pl.BlockSpec((None, tm, D_MODEL), lambda bi, i: (bi, i, 0)),
        out_shape=jax.ShapeDtypeStruct((b, s, D_MODEL), F32),
        scratch_shapes=[pltpu.VMEM((tm, D_FF), BF16)],
        compiler_params=_cparams(2),
        name="ffn",
    )(x, x, x, mod, g, wu, wc, wd, gf)


def _rope_tables(n):
    rows = n // GRID_W
    row = jnp.broadcast_to(jnp.arange(rows, dtype=F32)[:, None], (rows, GRID_W)).reshape(n)
    col = jnp.broadcast_to(jnp.arange(GRID_W, dtype=F32)[None, :], (rows, GRID_W)).reshape(n)
    half = HEAD_DIM // 2
    inv = ROPE_THETA ** (-jnp.arange(0, half, 2, dtype=F32) / half)
    ang = jnp.concatenate([row[:, None] * inv, col[:, None] * inv], axis=-1)
    cos, sin = jnp.cos(ang), jnp.sin(ang)
    zero = jnp.zeros_like(sin)
    reps = LANES // HEAD_DIM
    cos_t = jnp.tile(jnp.concatenate([cos, cos], axis=-1), (1, reps))
    sin_hi = jnp.tile(jnp.concatenate([zero, sin], axis=-1), (1, reps))
    sin_lo = jnp.tile(jnp.concatenate([-sin, zero], axis=-1), (1, reps))
    return cos_t, sin_hi, sin_lo


def kernel(x, c, ctx, c_ctx, w_mod, b_mod, g_mix, w_in, b_gate, sink, w_spatial, b_spatial, g_v, w_sconv,
           w_branch, w_out, g_ffn, w_up, w_fconv, w_down, g_final):
    b, n, _ = x.shape
    m = ctx.shape[1]
    assert b + 1 <= MOD_ROWS and n % 512 == 0 and m % CHUNK == 0

    c_all = jnp.zeros((MOD_ROWS, D_MODEL), F32).at[:b].set(c).at[CTX_ROW].set(c_ctx)
    mod = _modulation(c_all, w_mod, b_mod).reshape(DEPTH, MOD_ROWS, 1, 6 * D_MODEL)
    rope_tabs = _rope_tables(n)

    w_in_b = w_in.astype(BF16)
    w_branch_b = w_branch.astype(BF16)
    w_out_b = w_out.astype(BF16)
    w_up_b = w_up.astype(BF16)
    w_down_b = w_down.astype(BF16)
    w_sp_b = w_spatial.astype(BF16)
    bs_full = jnp.repeat(jnp.transpose(b_spatial, (0, 2, 1)), A_GW, axis=2)

    lat_row = lambda bi: bi
    ctx_row = lambda bi: CTX_ROW
    full_sections = ((OFF_Q, Q_W, Q_W, QK_SCALE, True), (OFF_K, KV_W, KV_W, 1.0, False),
                     (OFF_V, KV_W, 0, 1.0, True), (OFF_A, 2 * A_W, 0, 1.0, False),
                     (OFF_B, 3 * B_W, 0, 1.0, False), (OFF_G, N_BRANCH * D_MODEL, 0, 1.0, False))
    ctx_sections = tuple((s0, wd, 0, sc, tr) for (s0, wd, _, sc, tr) in full_sections)
    ctx_kv_sections = ((0, KV_W, 0, 1.0, False), (KV_W, KV_W, 0, 1.0, True))

    xc = ctx
    for l in range(DEPTH):
        last = l == DEPTH - 1
        mod_l = mod[l]
        g_mix_l = g_mix[l].reshape(1, D_MODEL)
        merge_w = (w_sp_b[l], bs_full[l], g_v[l].reshape(1, A_W), w_sconv[l], b_gate[l].reshape(1, -1),
                   w_branch_b[l], w_out_b[l])
        ffn_w = (g_ffn[l].reshape(1, D_MODEL), w_up_b[l], w_fconv[l], w_down_b[l], g_final.reshape(1, D_MODEL))

        q_t, k, v_t, za, zb, zg = _inproj(x, mod_l, lat_row, g_mix_l, w_in_b[l], full_sections, 256, rope_tabs)
        if last:
            kc, vc_t = _inproj(xc, mod_l, ctx_row, g_mix_l, w_in_b[l][:, OFF_K:OFF_A], ctx_kv_sections, m)
        else:
            qc_t, kc, vc_t, zc_a, zc_b, zc_g = _inproj(xc, mod_l, ctx_row, g_mix_l, w_in_b[l], ctx_sections, m)
        y_t = _attention(q_t, kc, vc_t, sink[l], k=k, v_t=v_t)
        x = _merge(x, mod_l, lat_row, y_t, za, zb, zg, *merge_w, tm=256)
        x = _ffn(x, mod_l, lat_row, *ffn_w, tm=512, final=last)
        if not last:
            yc_t = _attention(qc_t, kc, vc_t, sink[l])
            xc = _merge(xc, mod_l, ctx_row, yc_t, zc_a, zc_b, zc_g, *merge_w, tm=m)
            xc = _ffn(xc, mod_l, ctx_row, *ffn_w, tm=m, final=False)
    return x
```

```python
import functools

import jax
import jax.numpy as jnp
from jax import lax
from jax.experimental import pallas as pl
from jax.experimental.pallas import tpu as pltpu

F32 = jnp.float32
BF16 = jnp.bfloat16

D_MODEL = 1024
DEPTH = 2
GRID_W = 64
N_HEADS = 16
N_KV_HEADS = 4
HEAD_DIM = 64
Q_W = N_HEADS * HEAD_DIM
KV_W = N_KV_HEADS * HEAD_DIM
BLK = 128
ROPE_THETA = 10000.0
CHUNK = 128
A_W = D_MODEL
A_GROUPS = 8
A_GW = A_W // A_GROUPS
B_W = D_MODEL
N_BRANCH = 3
D_FF = 2816
EPS = 1e-6
NEG = -1e30
OFF_Q = 0
OFF_K = OFF_Q + Q_W
OFF_V = OFF_K + KV_W
OFF_A = OFF_V + KV_W
OFF_B = OFF_A + 2 * A_W
OFF_G = OFF_B + 3 * B_W
IN_W = OFF_G + N_BRANCH * D_MODEL
MIX_A, MIX_BG, MIX_P, MIX_G = 0, OFF_B - OFF_A, OFF_B - OFF_A + B_W, OFF_G - OFF_A

LOG2E = 1.4426950408889634
QK_SCALE = HEAD_DIM ** -0.5 * LOG2E

LANES = 128
HALO = 16
VMEM_LIMIT = 56 * 1024 * 1024
MOD_ROWS = 8
CTX_ROW = 4


def _cparams(n_axes):
    return pltpu.CompilerParams(
        dimension_semantics=("arbitrary",) * n_axes, vmem_limit_bytes=VMEM_LIMIT)


def _rms(x):
    return x * lax.rsqrt(jnp.mean(x * x, axis=-1, keepdims=True) + EPS)


def _layer_spec(a, l):
    return pl.BlockSpec((None,) + a.shape[1:], lambda bi, i: (l,) + (0,) * (a.ndim - 1),
                        pipeline_mode=pl.Buffered(1))


def _mod_spec(l, mod_row):
    return pl.BlockSpec((None, None, 1, 6 * D_MODEL), lambda bi, i: (l, mod_row(bi), 0, 0))


def _halo_specs(tm, s):
    per_tile, n_halo = tm // HALO, s // HALO
    return [
        pl.BlockSpec((None, HALO, D_MODEL), lambda bi, i: (bi, jnp.maximum(i * per_tile - 1, 0), 0)),
        pl.BlockSpec((None, HALO, D_MODEL), lambda bi, i: (bi, jnp.minimum((i + 1) * per_tile, n_halo - 1), 0)),
    ]


def _mod_kernel(c_ref, w_ref, b_ref, o_ref):
    a = jax.nn.silu(c_ref[...]).astype(BF16)
    o_ref[...] = jnp.dot(a, w_ref[...].astype(BF16), preferred_element_type=F32) + b_ref[...]


def _modulation(c_all, w_mod, b_mod):
    tn = 1536
    n_out = 6 * D_MODEL
    return pl.pallas_call(
        _mod_kernel,
        grid=(DEPTH, n_out // tn),
        in_specs=[
            pl.BlockSpec((MOD_ROWS, D_MODEL), lambda l, j: (0, 0)),
            pl.BlockSpec((None, D_MODEL, tn), lambda l, j: (l, 0, j)),
            pl.BlockSpec((None, 1, tn), lambda l, j: (l, 0, j)),
        ],
        out_specs=pl.BlockSpec((None, MOD_ROWS, tn), lambda l, j: (l, 0, j)),
        out_shape=jax.ShapeDtypeStruct((DEPTH, MOD_ROWS, n_out), F32),
        compiler_params=_cparams(2),
        name="modulation",
    )(c_all, w_mod, b_mod.reshape(DEPTH, 1, n_out))


def _inproj_kernel(*refs, sections, rope, tn):
    if rope:
        x_ref, mod_ref, g_ref, cos_ref, sa_ref, sb_ref, w_ref = refs[:7]
        out_refs = refs[7:]
    else:
        x_ref, mod_ref, g_ref, w_ref = refs[:4]
        out_refs = refs[4:]
    h = _rms(x_ref[...]) * g_ref[...]
    h = h * (1.0 + mod_ref[:, D_MODEL:2 * D_MODEL]) + mod_ref[:, 0:D_MODEL]
    hb = h.astype(BF16)
    for (start, width, rope_w, scale, tile), o_ref in zip(sections, out_refs):
        for c0 in range(0, width, tn):
            w = min(tn, width - c0)
            acc = jnp.dot(hb, w_ref[:, start + c0:start + c0 + w], preferred_element_type=F32)
            for cc in range(0, w, LANES):
                t = acc[:, cc:cc + LANES]
                if rope and c0 + cc < rope_w:
                    t = (t * cos_ref[...] + pltpu.roll(t, 32, 1) * sa_ref[...]
                         + pltpu.roll(t, LANES - 32, 1) * sb_ref[...])
                if scale != 1.0:
                    t = t * scale
                if tile:
                    for jj in range(t.shape[0] // tile):
                        o_ref[jj, c0 + cc:c0 + cc + LANES, :] = jnp.transpose(
                            t[jj * tile:(jj + 1) * tile, :]).astype(BF16)
                else:
                    o_ref[:, c0 + cc:c0 + cc + LANES] = t.astype(BF16)


def _inproj(l, x, mod, mod_row, g, w, sections, tm, rope_tabs=None):
    b, s, _ = x.shape
    nt = s // tm
    rope = rope_tabs is not None
    in_specs = [
        pl.BlockSpec((None, tm, D_MODEL), lambda bi, i: (bi, i, 0)),
        _mod_spec(l, mod_row),
        _layer_spec(g, l),
    ]
    args = [x, mod, g]
    if rope:
        in_specs += [pl.BlockSpec((tm, LANES), lambda bi, i: (i, 0))] * 3
        args += list(rope_tabs)
    in_specs.append(_layer_spec(w, l))
    args.append(w)
    outs = pl.pallas_call(
        functools.partial(_inproj_kernel, sections=tuple(sections), rope=rope, tn=512),
        grid=(b, nt),
        in_specs=in_specs,
        out_specs=[pl.BlockSpec((None, tm // sec[4], sec[1], sec[4]), lambda bi, i: (bi, i, 0, 0)) if sec[4]
                   else pl.BlockSpec((None, tm, sec[1]), lambda bi, i: (bi, i, 0)) for sec in sections],
        out_shape=[jax.ShapeDtypeStruct((b, s // sec[4], sec[1], sec[4]) if sec[4] else (b, s, sec[1]), BF16)
                   for sec in sections],
        compiler_params=_cparams(2),
        name="inproj",
    )(*args)
    return outs


def _attn_kernel(*refs, l, local, nb):
    if local:
        qt_ref, kp_ref, km_ref, kn_ref, vp_ref, vm_ref, vn_ref, kc_ref, vc_ref, sink_ref, yt_ref = refs
    else:
        qt_ref, kc_ref, vc_ref, sink_ref, yt_ref = refs
    nsub = qt_ref.shape[1] // BLK
    ty = yt_ref.shape[2]
    grp = N_HEADS // N_KV_HEADS
    width = grp * BLK
    i = pl.program_id(1)
    if local:
        kj = lax.broadcasted_iota(jnp.int32, (BLK, width), 0)
        qi = lax.broadcasted_iota(jnp.int32, (BLK, width), 1) & (BLK - 1)
        band_prev, band_next = kj >= qi, kj <= qi
        edge_prev = kj >= qi + jnp.where(i > 0, 0, BLK)
        edge_next = kj <= qi - jnp.where(i < nb - 1, 0, BLK)
    zeros = jnp.zeros((HEAD_DIM, width), BF16)

    def key_chunks(j, lanes):
        ks = [kc_ref[:, lanes]]
        if local:
            blk = lambda t: slice(t * BLK, (t + 1) * BLK)
            ks.append(kp_ref[:, lanes] if j == 0 else km_ref[blk(j - 1), lanes])
            ks.append(km_ref[blk(j), lanes])
            ks.append(kn_ref[:, lanes] if j == nsub - 1 else km_ref[blk(j + 1), lanes])
        return ks

    def value_chunks(j, rows):
        vs = [vc_ref[t, rows, :] for t in range(vc_ref.shape[0])]
        if local:
            vs.append(vp_ref[rows, :] if j == 0 else vm_ref[j - 1, rows, :])
            vs.append(vm_ref[j, rows, :])
            vs.append(vn_ref[rows, :] if j == nsub - 1 else vm_ref[j + 1, rows, :])
        return vs

    def scores(j, kh):
        pair, half = kh // 2, kh % 2
        cols = slice(j * BLK, (j + 1) * BLK)
        q4 = jnp.concatenate(
            [qt_ref[(kh * grp + r) * HEAD_DIM:(kh * grp + r + 1) * HEAD_DIM, cols] for r in range(grp)], axis=1)
        q_pad = jnp.concatenate([q4, zeros] if half == 0 else [zeros, q4], axis=0)
        ss = [jnp.dot(kk, q_pad, preferred_element_type=F32)
              for kk in key_chunks(j, slice(pair * LANES, (pair + 1) * LANES))]
        if local:
            ss[1] = jnp.where(edge_prev if j == 0 else band_prev, ss[1], NEG)
            ss[3] = jnp.where(edge_next if j == nsub - 1 else band_next, ss[3], NEG)
        return ss

    def finish(j, kh, ss):
        sink = jnp.concatenate(
            [jnp.full((1, BLK), sink_ref[l, kh * grp + r] * LOG2E, F32) for r in range(grp)], axis=1)
        m = sink
        for s_x in ss:
            m = jnp.maximum(m, jnp.max(s_x, axis=0, keepdims=True))
        p_t = jnp.concatenate([jnp.exp2(s_x - m).astype(BF16) for s_x in ss], axis=0)
        v_t = jnp.concatenate(value_chunks(j, slice(kh * HEAD_DIM, (kh + 1) * HEAD_DIM)), axis=1)
        v_ext = jnp.concatenate([v_t, jnp.ones((HALO, v_t.shape[1]), BF16)], axis=0)
        o_ext = jnp.dot(v_ext, p_t, preferred_element_type=F32)
        den = o_ext[HEAD_DIM:HEAD_DIM + 1, :] + jnp.exp2(sink - m)
        o_t = o_ext[:HEAD_DIM, :] / den
        for r in range(grp):
            h = kh * grp + r
            t0 = j * BLK
            yt_ref[t0 // ty, h * HEAD_DIM:(h + 1) * HEAD_DIM, t0 % ty:t0 % ty + BLK] = (
                o_t[:, r * BLK:(r + 1) * BLK].astype(BF16))

    items = [(j, kh) for j in range(nsub) for kh in range(N_KV_HEADS)]
    ss_next = scores(*items[0])
    for n, (j, kh) in enumerate(items):
        ss = ss_next
        if n + 1 < len(items):
            ss_next = scores(*items[n + 1])
        finish(j, kh, ss)


def _attention(l, q_t, kc, vc_t, sink, ty, k=None, v_t=None):
    b, nt, _, tq = q_t.shape
    s = nt * tq
    m = kc.shape[1]
    local = k is not None
    nsub, nblk = tq // BLK, s // BLK
    prev_b = lambda i: jnp.maximum(i * nsub - 1, 0)
    next_b = lambda i: jnp.minimum((i + 1) * nsub, nblk - 1)
    in_specs = [pl.BlockSpec((None, None, Q_W, tq), lambda bi, i: (bi, i, 0, 0))]
    args = [q_t]
    if local:
        in_specs += [
            pl.BlockSpec((None, BLK, KV_W), lambda bi, i: (bi, prev_b(i), 0)),
            pl.BlockSpec((None, tq, KV_W), lambda bi, i: (bi, i, 0)),
            pl.BlockSpec((None, BLK, KV_W), lambda bi, i: (bi, next_b(i), 0)),
            pl.BlockSpec((None, None, KV_W, BLK), lambda bi, i: (bi, prev_b(i), 0, 0)),
            pl.BlockSpec((None, nsub, KV_W, BLK), lambda bi, i: (bi, i, 0, 0)),
            pl.BlockSpec((None, None, KV_W, BLK), lambda bi, i: (bi, next_b(i), 0, 0)),
        ]
        args += [k, k, k, v_t, v_t, v_t]
    in_specs += [
        pl.BlockSpec((None, m, KV_W), lambda bi, i: (bi, 0, 0)),
        pl.BlockSpec((None, m // BLK, KV_W, BLK), lambda bi, i: (bi, 0, 0, 0)),
        pl.BlockSpec(memory_space=pltpu.SMEM),
    ]
    args += [kc, vc_t, sink]
    return pl.pallas_call(
        functools.partial(_attn_kernel, l=l, local=local, nb=nt),
        grid=(b, nt),
        in_specs=in_specs,
        out_specs=pl.BlockSpec((None, tq // ty, Q_W, ty), lambda bi, i: (bi, i, 0, 0)),
        out_shape=jax.ShapeDtypeStruct((b, s // ty, Q_W, ty), BF16),
        compiler_params=_cparams(2),
        name="attention",
    )(*args)


def _mix_kernel(x_ref, xp_ref, xn_ref, mod_ref, g_ref, yt_ref, w_ref, ws_ref, bs_ref, gv_ref, wc_ref,
                bg_ref, wb_ref, wo_ref, o_ref, ya_ref, *, nt):
    tm = x_ref.shape[0]
    i = pl.program_id(1)
    sh, sc = mod_ref[:, 0:D_MODEL], mod_ref[:, D_MODEL:2 * D_MODEL]

    def hmod(x):
        return ((_rms(x) * g_ref[...]) * (1.0 + sc) + sh).astype(BF16)

    x = x_ref[...]
    hm = hmod(x)
    he = jnp.concatenate([hmod(xp_ref[...]), hm, hmod(xn_ref[...])], axis=0)
    y_attn = lax.dot_general(yt_ref[...], wb_ref[0], (((0,), (0,)), ((), ())), preferred_element_type=F32)
    z_a = jnp.dot(hm, w_ref[:, MIX_A:MIX_BG], preferred_element_type=F32)
    z_p = jnp.dot(he, w_ref[:, MIX_P:MIX_G], preferred_element_type=F32)
    z_bg = jnp.dot(hm, w_ref[:, MIX_BG:MIX_P], preferred_element_type=F32)
    for n in range(tm // CHUNK):
        rows = slice(n * CHUNK, (n + 1) * CHUNK)
        z = jax.nn.gelu(z_a[rows, :])
        u, v = z[:, :A_W], z[:, A_W:]
        vn = (_rms(v) * gv_ref[...]).astype(BF16)
        for g in range(A_GROUPS):
            cols = slice(g * A_GW, (g + 1) * A_GW)
            mixed = jnp.dot(ws_ref[g], vn[:, cols], preferred_element_type=F32) + bs_ref[:, cols]
            ya_ref[rows, cols] = (u[:, cols] * mixed).astype(BF16)
    z_g = jnp.dot(hm, w_ref[:, MIX_G:], preferred_element_type=F32)
    prod = z_p[:, :B_W] * z_p[:, B_W:]
    r = lax.broadcasted_iota(jnp.int32, (tm, B_W), 0)
    first = r < jnp.where(i == 0, 1, 0)
    last = r >= tm - jnp.where(i == nt - 1, 1, 0)
    pm1 = jnp.where(first, 0.0, pltpu.roll(prod, 1, 0)[HALO:HALO + tm])
    pp1 = jnp.where(last, 0.0, pltpu.roll(prod, tm + 2 * HALO - 1, 0)[HALO:HALO + tm])
    y_b = z_bg * (wc_ref[0:1, :] * pm1 + wc_ref[1:2, :] * prod[HALO:HALO + tm] + wc_ref[2:3, :] * pp1)
    p_a = jnp.dot(ya_ref[...], wb_ref[1], preferred_element_type=F32)
    gates = jax.nn.sigmoid(z_g + bg_ref[...])
    p_b = jnp.dot(y_b.astype(BF16), wb_ref[2], preferred_element_type=F32)
    merged = (gates[:, :D_MODEL] * y_attn + gates[:, D_MODEL:2 * D_MODEL] * p_a + gates[:, 2 * D_MODEL:] * p_b)
    out = jnp.dot(merged.astype(BF16), wo_ref[...], preferred_element_type=F32)
    o_ref[...] = x + mod_ref[:, 2 * D_MODEL:3 * D_MODEL] * out


def _mix(l, x, mod, mod_row, y_t, g, w_abg, ws, bs, gv, wc, bg, wb, wo, tm):
    b, s, _ = x.shape
    nt = s // tm
    params = (w_abg, ws, bs, gv, wc, bg, wb, wo)
    in_specs = [pl.BlockSpec((None, tm, D_MODEL), lambda bi, i: (bi, i, 0))] + _halo_specs(tm, s) + [
        _mod_spec(l, mod_row),
        _layer_spec(g, l),
        pl.BlockSpec((None, None, Q_W, tm), lambda bi, i: (bi, i, 0, 0)),
    ] + [_layer_spec(a, l) for a in params]
    return pl.pallas_call(
        functools.partial(_mix_kernel, nt=nt),
        grid=(b, nt),
        in_specs=in_specs,
        out_specs=pl.BlockSpec((None, tm, D_MODEL), lambda bi, i: (bi, i, 0)),
        out_shape=jax.ShapeDtypeStruct((b, s, D_MODEL), F32),
        scratch_shapes=[pltpu.VMEM((tm, A_W), BF16)],
        compiler_params=_cparams(2),
        name="mix",
    )(x, x, x, mod, g, y_t, *params)


def _ffn_kernel(x_ref, xp_ref, xn_ref, mod_ref, g_ref, wu_ref, wc_ref, wd_ref, gf_ref, o_ref, act_ref,
                *, nt, final, fc):
    tm = x_ref.shape[0]
    i = pl.program_id(1)
    sh, sc = mod_ref[:, 3 * D_MODEL:4 * D_MODEL], mod_ref[:, 4 * D_MODEL:5 * D_MODEL]

    def hmod(x):
        return ((_rms(x) * g_ref[...]) * (1.0 + sc) + sh).astype(BF16)

    x = x_ref[...]
    hm = hmod(x)
    he = jnp.concatenate([hmod(xp_ref[...]), hm, hmod(xn_ref[...])], axis=0)
    r = lax.broadcasted_iota(jnp.int32, (tm, fc), 0)
    first = r < jnp.where(i == 0, 1, 0)
    last = r >= tm - jnp.where(i == nt - 1, 1, 0)
    for c in range(0, D_FF, fc):
        a = jnp.dot(he, wu_ref[:, c:c + fc], preferred_element_type=F32)
        gt = jnp.dot(hm, wu_ref[:, D_FF + c:D_FF + c + fc], preferred_element_type=F32)
        am1 = jnp.where(first, 0.0, pltpu.roll(a, 1, 0)[HALO:HALO + tm])
        ap1 = jnp.where(last, 0.0, pltpu.roll(a, tm + 2 * HALO - 1, 0)[HALO:HALO + tm])
        conv = (wc_ref[0:1, c:c + fc] * am1 + wc_ref[1:2, c:c + fc] * a[HALO:HALO + tm]
                + wc_ref[2:3, c:c + fc] * ap1)
        act_ref[:, c:c + fc] = (jax.nn.silu(conv) * gt).astype(BF16)
    y = jnp.dot(act_ref[...], wd_ref[...], preferred_element_type=F32)
    out = x + mod_ref[:, 5 * D_MODEL:6 * D_MODEL] * y
    if final:
        out = _rms(out) * gf_ref[...]
    o_ref[...] = out


def _ffn(l, x, mod, mod_row, g, wu, wc, wd, gf, tm, final):
    b, s, _ = x.shape
    nt = s // tm
    in_specs = [pl.BlockSpec((None, tm, D_MODEL), lambda bi, i: (bi, i, 0))] + _halo_specs(tm, s) + [
        _mod_spec(l, mod_row),
        _layer_spec(g, l), _layer_spec(wu, l), _layer_spec(wc, l), _layer_spec(wd, l),
        pl.BlockSpec(gf.shape, lambda bi, i: (0, 0)),
    ]
    return pl.pallas_call(
        functools.partial(_ffn_kernel, nt=nt, final=final, fc=256),
        grid=(b, nt),
        in_specs=in_specs,
        out_specs=pl.BlockSpec((None, tm, D_MODEL), lambda bi, i: (bi, i, 0)),
        out_shape=jax.ShapeDtypeStruct((b, s, D_MODEL), F32),
        scratch_shapes=[pltpu.VMEM((tm, D_FF), BF16)],
        compiler_params=_cparams(2),
        name="ffn",
    )(x, x, x, mod, g, wu, wc, wd, gf)


def _rope_tables(n):
    rows = n // GRID_W
    row = jnp.broadcast_to(jnp.arange(rows, dtype=F32)[:, None], (rows, GRID_W)).reshape(n)
    col = jnp.broadcast_to(jnp.arange(GRID_W, dtype=F32)[None, :], (rows, GRID_W)).reshape(n)
    half = HEAD_DIM // 2
    inv = ROPE_THETA ** (-jnp.arange(0, half, 2, dtype=F32) / half)
    ang = jnp.concatenate([row[:, None] * inv, col[:, None] * inv], axis=-1)
    cos, sin = jnp.cos(ang), jnp.sin(ang)
    zero = jnp.zeros_like(sin)
    reps = LANES // HEAD_DIM
    cos_t = jnp.tile(jnp.concatenate([cos, cos], axis=-1), (1, reps))
    sin_hi = jnp.tile(jnp.concatenate([zero, sin], axis=-1), (1, reps))
    sin_lo = jnp.tile(jnp.concatenate([-sin, zero], axis=-1), (1, reps))
    return cos_t, sin_hi, sin_lo


def kernel(x, c, ctx, c_ctx, w_mod, b_mod, g_mix, w_in, b_gate, sink, w_spatial, b_spatial, g_v, w_sconv,
           w_branch, w_out, g_ffn, w_up, w_fconv, w_down, g_final):
    b, n, _ = x.shape
    m = ctx.shape[1]
    assert b + 1 <= MOD_ROWS and n % 512 == 0 and m % CHUNK == 0

    c_all = jnp.zeros((MOD_ROWS, D_MODEL), F32).at[:b].set(c).at[CTX_ROW].set(c_ctx)
    mod = _modulation(c_all, w_mod, b_mod).reshape(DEPTH, MOD_ROWS, 1, 6 * D_MODEL)
    rope_tabs = _rope_tables(n)

    row = lambda a: a.reshape(DEPTH, 1, -1)
    g_mix_r = row(g_mix)
    w_qkv_b = w_in[:, :, :OFF_A].astype(BF16)
    bs_full = jnp.repeat(jnp.transpose(b_spatial, (0, 2, 1)), A_GW, axis=2)
    mix_w = (g_mix_r, w_in[:, :, OFF_A:].astype(BF16), w_spatial.astype(BF16), bs_full, row(g_v), w_sconv,
             row(b_gate), w_branch.astype(BF16), w_out.astype(BF16))
    ffn_w = (row(g_ffn), w_up.astype(BF16), w_fconv, w_down.astype(BF16), g_final.reshape(1, D_MODEL))

    lat_row = lambda bi: bi
    ctx_row = lambda bi: CTX_ROW
    tq, tmix = 512, 256
    qkv_sections = lambda t: ((OFF_Q, Q_W, Q_W, QK_SCALE, t), (OFF_K, KV_W, KV_W, 1.0, 0),
                              (OFF_V, KV_W, 0, 1.0, BLK))

    xc = ctx
    for l in range(DEPTH):
        last = l == DEPTH - 1
        q_t, k, v_t = _inproj(l, x, mod, lat_row, g_mix_r, w_qkv_b, qkv_sections(tq), tq, rope_tabs)
        ctx_out = _inproj(l, xc, mod, ctx_row, g_mix_r, w_qkv_b, qkv_sections(m)[1 if last else 0:], m)
        kc, vc_t = ctx_out[-2:]
        y_t = _attention(l, q_t, kc, vc_t, sink, tmix, k=k, v_t=v_t)
        x = _mix(l, x, mod, lat_row, y_t, *mix_w, tm=tmix)
        x = _ffn(l, x, mod, lat_row, *ffn_w, tm=512, final=last)
        if not last:
            yc_t = _attention(l, ctx_out[0], kc, vc_t, sink, m)
            xc = _mix(l, xc, mod, ctx_row, yc_t, *mix_w, tm=m)
            xc = _ffn(l, xc, mod, ctx_row, *ffn_w, tm=m, final=False)
    return x
```

```python
import functools

import jax
import jax.numpy as jnp
from jax import lax
from jax.experimental import pallas as pl
from jax.experimental.pallas import tpu as pltpu

F32 = jnp.float32
BF16 = jnp.bfloat16

D_MODEL = 1024
DEPTH = 2
GRID_W = 64
N_HEADS = 16
N_KV_HEADS = 4
HEAD_DIM = 64
Q_W = N_HEADS * HEAD_DIM
KV_W = N_KV_HEADS * HEAD_DIM
BLK = 128
ROPE_THETA = 10000.0
CHUNK = 128
A_W = D_MODEL
A_GROUPS = 8
A_GW = A_W // A_GROUPS
B_W = D_MODEL
N_BRANCH = 3
D_FF = 2816
EPS = 1e-6
NEG = -1e30
OFF_Q = 0
OFF_K = OFF_Q + Q_W
OFF_V = OFF_K + KV_W
OFF_A = OFF_V + KV_W
OFF_B = OFF_A + 2 * A_W
OFF_G = OFF_B + 3 * B_W
IN_W = OFF_G + N_BRANCH * D_MODEL
MIX_A, MIX_BG, MIX_P, MIX_G = OFF_A, OFF_B, OFF_B + B_W, OFF_G

LOG2E = 1.4426950408889634
QK_SCALE = HEAD_DIM ** -0.5 * LOG2E

LANES = 128
HALO = 16
VMEM_LIMIT = 56 * 1024 * 1024
MOD_ROWS = 8
CTX_ROW = 4


def _cparams(n_axes):
    return pltpu.CompilerParams(
        dimension_semantics=("arbitrary",) * n_axes, vmem_limit_bytes=VMEM_LIMIT)


def _rms(x):
    return x * lax.rsqrt(jnp.mean(x * x, axis=-1, keepdims=True) + EPS)


def _layer_spec(a, l):
    return pl.BlockSpec((None,) + a.shape[1:], lambda bi, i: (l,) + (0,) * (a.ndim - 1),
                        pipeline_mode=pl.Buffered(1))


def _mod_spec(l, mod_row):
    return pl.BlockSpec((None, None, 1, 6 * D_MODEL), lambda bi, i: (l, mod_row(bi), 0, 0))


def _halo_specs(tm, s):
    per_tile, n_halo = tm // HALO, s // HALO
    return [
        pl.BlockSpec((None, HALO, D_MODEL), lambda bi, i: (bi, jnp.maximum(i * per_tile - 1, 0), 0)),
        pl.BlockSpec((None, HALO, D_MODEL), lambda bi, i: (bi, jnp.minimum((i + 1) * per_tile, n_halo - 1), 0)),
    ]


def _mod_kernel(c_ref, w_ref, b_ref, o_ref):
    a = jax.nn.silu(c_ref[...]).astype(BF16)
    o_ref[...] = jnp.dot(a, w_ref[...].astype(BF16), preferred_element_type=F32) + b_ref[...]


def _modulation(c_all, w_mod, b_mod):
    tn = 1536
    n_out = 6 * D_MODEL
    return pl.pallas_call(
        _mod_kernel,
        grid=(DEPTH, n_out // tn),
        in_specs=[
            pl.BlockSpec((MOD_ROWS, D_MODEL), lambda l, j: (0, 0)),
            pl.BlockSpec((None, D_MODEL, tn), lambda l, j: (l, 0, j)),
            pl.BlockSpec((None, 1, tn), lambda l, j: (l, 0, j)),
        ],
        out_specs=pl.BlockSpec((None, MOD_ROWS, tn), lambda l, j: (l, 0, j)),
        out_shape=jax.ShapeDtypeStruct((DEPTH, MOD_ROWS, n_out), F32),
        compiler_params=_cparams(2),
        name="modulation",
    )(c_all, w_mod, b_mod.reshape(DEPTH, 1, n_out))


def _inproj_kernel(*refs, sections, rope, tn):
    if rope:
        x_ref, mod_ref, g_ref, cos_ref, sa_ref, sb_ref, w_ref = refs[:7]
        out_refs = refs[7:]
    else:
        x_ref, mod_ref, g_ref, w_ref = refs[:4]
        out_refs = refs[4:]
    h = _rms(x_ref[...]) * g_ref[...]
    h = h * (1.0 + mod_ref[:, D_MODEL:2 * D_MODEL]) + mod_ref[:, 0:D_MODEL]
    hb = h.astype(BF16)
    for (start, width, rope_w, scale, tile), o_ref in zip(sections, out_refs):
        for c0 in range(0, width, tn):
            w = min(tn, width - c0)
            acc = jnp.dot(hb, w_ref[:, start + c0:start + c0 + w], preferred_element_type=F32)
            for cc in range(0, w, LANES):
                t = acc[:, cc:cc + LANES]
                if rope and c0 + cc < rope_w:
                    t = (t * cos_ref[...] + pltpu.roll(t, 32, 1) * sa_ref[...]
                         + pltpu.roll(t, LANES - 32, 1) * sb_ref[...])
                if scale != 1.0:
                    t = t * scale
                if tile:
                    for jj in range(t.shape[0] // tile):
                        o_ref[jj, c0 + cc:c0 + cc + LANES, :] = jnp.transpose(
                            t[jj * tile:(jj + 1) * tile, :]).astype(BF16)
                else:
                    o_ref[:, c0 + cc:c0 + cc + LANES] = t.astype(BF16)


def _inproj(l, x, mod, mod_row, g, w, sections, tm, rope_tabs=None):
    b, s, _ = x.shape
    nt = s // tm
    rope = rope_tabs is not None
    in_specs = [
        pl.BlockSpec((None, tm, D_MODEL), lambda bi, i: (bi, i, 0)),
        _mod_spec(l, mod_row),
        _layer_spec(g, l),
    ]
    args = [x, mod, g]
    if rope:
        in_specs += [pl.BlockSpec((tm, LANES), lambda bi, i: (i, 0))] * 3
        args += list(rope_tabs)
    in_specs.append(pl.BlockSpec((None, D_MODEL, OFF_A), lambda bi, i: (l, 0, 0), pipeline_mode=pl.Buffered(1)))
    args.append(w)
    outs = pl.pallas_call(
        functools.partial(_inproj_kernel, sections=tuple(sections), rope=rope, tn=512),
        grid=(b, nt),
        in_specs=in_specs,
        out_specs=[pl.BlockSpec((None, tm // sec[4], sec[1], sec[4]), lambda bi, i: (bi, i, 0, 0)) if sec[4]
                   else pl.BlockSpec((None, tm, sec[1]), lambda bi, i: (bi, i, 0)) for sec in sections],
        out_shape=[jax.ShapeDtypeStruct((b, s // sec[4], sec[1], sec[4]) if sec[4] else (b, s, sec[1]), BF16)
                   for sec in sections],
        compiler_params=_cparams(2),
        name="inproj",
    )(*args)
    return outs


def _attn_kernel(*refs, l, local, nb):
    if local:
        qt_ref, kp_ref, km_ref, kn_ref, vp_ref, vm_ref, vn_ref, kc_ref, vc_ref, sink_ref, yt_ref = refs
    else:
        qt_ref, kc_ref, vc_ref, sink_ref, yt_ref = refs
    nsub = qt_ref.shape[1] // BLK
    ty = yt_ref.shape[2]
    grp = N_HEADS // N_KV_HEADS
    width = grp * BLK
    i = pl.program_id(1)
    if local:
        kj = lax.broadcasted_iota(jnp.int32, (BLK, width), 0)
        qi = lax.broadcasted_iota(jnp.int32, (BLK, width), 1) & (BLK - 1)
        band_prev, band_next = kj >= qi, kj <= qi
        edge_prev = kj >= qi + jnp.where(i > 0, 0, BLK)
        edge_next = kj <= qi - jnp.where(i < nb - 1, 0, BLK)
    zeros = jnp.zeros((HEAD_DIM, width), BF16)

    def key_chunks(j, lanes):
        ks = [kc_ref[:, lanes]]
        if local:
            blk = lambda t: slice(t * BLK, (t + 1) * BLK)
            ks.append(kp_ref[:, lanes] if j == 0 else km_ref[blk(j - 1), lanes])
            ks.append(km_ref[blk(j), lanes])
            ks.append(kn_ref[:, lanes] if j == nsub - 1 else km_ref[blk(j + 1), lanes])
        return ks

    def value_chunks(j, rows):
        vs = [vc_ref[t, rows, :] for t in range(vc_ref.shape[0])]
        if local:
            vs.append(vp_ref[rows, :] if j == 0 else vm_ref[j - 1, rows, :])
            vs.append(vm_ref[j, rows, :])
            vs.append(vn_ref[rows, :] if j == nsub - 1 else vm_ref[j + 1, rows, :])
        return vs

    def scores(j, kh):
        pair, half = kh // 2, kh % 2
        cols = slice(j * BLK, (j + 1) * BLK)
        q4 = jnp.concatenate(
            [qt_ref[(kh * grp + r) * HEAD_DIM:(kh * grp + r + 1) * HEAD_DIM, cols] for r in range(grp)], axis=1)
        q_pad = jnp.concatenate([q4, zeros] if half == 0 else [zeros, q4], axis=0)
        ss = [jnp.dot(kk, q_pad, preferred_element_type=F32)
              for kk in key_chunks(j, slice(pair * LANES, (pair + 1) * LANES))]
        if local:
            ss[1] = jnp.where(edge_prev if j == 0 else band_prev, ss[1], NEG)
            ss[3] = jnp.where(edge_next if j == nsub - 1 else band_next, ss[3], NEG)
        return ss

    def finish(j, kh, ss):
        sink = jnp.concatenate(
            [jnp.full((1, BLK), sink_ref[l, kh * grp + r] * LOG2E, F32) for r in range(grp)], axis=1)
        m = sink
        for s_x in ss:
            m = jnp.maximum(m, jnp.max(s_x, axis=0, keepdims=True))
        p_t = jnp.concatenate([jnp.exp2(s_x - m).astype(BF16) for s_x in ss], axis=0)
        v_t = jnp.concatenate(value_chunks(j, slice(kh * HEAD_DIM, (kh + 1) * HEAD_DIM)), axis=1)
        v_ext = jnp.concatenate([v_t, jnp.ones((HALO, v_t.shape[1]), BF16)], axis=0)
        o_ext = jnp.dot(v_ext, p_t, preferred_element_type=F32)
        den = o_ext[HEAD_DIM:HEAD_DIM + 1, :] + jnp.exp2(sink - m)
        o_t = o_ext[:HEAD_DIM, :] / den
        for r in range(grp):
            h = kh * grp + r
            t0 = j * BLK
            yt_ref[t0 // ty, h * HEAD_DIM:(h + 1) * HEAD_DIM, t0 % ty:t0 % ty + BLK] = (
                o_t[:, r * BLK:(r + 1) * BLK].astype(BF16))

    items = [(j, kh) for j in range(nsub) for kh in range(N_KV_HEADS)]
    ss_next = scores(*items[0])
    for n, (j, kh) in enumerate(items):
        ss = ss_next
        if n + 1 < len(items):
            ss_next = scores(*items[n + 1])
        finish(j, kh, ss)


def _attention(l, q_t, kc, vc_t, sink, ty, k=None, v_t=None):
    b, nt, _, tq = q_t.shape
    s = nt * tq
    m = kc.shape[1]
    local = k is not None
    nsub, nblk = tq // BLK, s // BLK
    prev_b = lambda i: jnp.maximum(i * nsub - 1, 0)
    next_b = lambda i: jnp.minimum((i + 1) * nsub, nblk - 1)
    in_specs = [pl.BlockSpec((None, None, Q_W, tq), lambda bi, i: (bi, i, 0, 0))]
    args = [q_t]
    if local:
        in_specs += [
            pl.BlockSpec((None, BLK, KV_W), lambda bi, i: (bi, prev_b(i), 0)),
            pl.BlockSpec((None, tq, KV_W), lambda bi, i: (bi, i, 0)),
            pl.BlockSpec((None, BLK, KV_W), lambda bi, i: (bi, next_b(i), 0)),
            pl.BlockSpec((None, None, KV_W, BLK), lambda bi, i: (bi, prev_b(i), 0, 0)),
            pl.BlockSpec((None, nsub, KV_W, BLK), lambda bi, i: (bi, i, 0, 0)),
            pl.BlockSpec((None, None, KV_W, BLK), lambda bi, i: (bi, next_b(i), 0, 0)),
        ]
        args += [k, k, k, v_t, v_t, v_t]
    in_specs += [
        pl.BlockSpec((None, m, KV_W), lambda bi, i: (bi, 0, 0)),
        pl.BlockSpec((None, m // BLK, KV_W, BLK), lambda bi, i: (bi, 0, 0, 0)),
        pl.BlockSpec(memory_space=pltpu.SMEM),
    ]
    args += [kc, vc_t, sink]
    return pl.pallas_call(
        functools.partial(_attn_kernel, l=l, local=local, nb=nt),
        grid=(b, nt),
        in_specs=in_specs,
        out_specs=pl.BlockSpec((None, tq // ty, Q_W, ty), lambda bi, i: (bi, i, 0, 0)),
        out_shape=jax.ShapeDtypeStruct((b, s // ty, Q_W, ty), BF16),
        compiler_params=_cparams(2),
        name="attention",
    )(*args)


def _mix_kernel(x_ref, xp_ref, xn_ref, mod_ref, g_ref, yt_ref, w_ref, ws_ref, bs_ref, gv_ref, wc_ref,
                bg_ref, wb_ref, wo_ref, o_ref, ya_ref, *, nt):
    tm = x_ref.shape[0]
    i = pl.program_id(1)
    sh, sc = mod_ref[:, 0:D_MODEL], mod_ref[:, D_MODEL:2 * D_MODEL]

    def hmod(x):
        return ((_rms(x) * g_ref[...]) * (1.0 + sc) + sh).astype(BF16)

    x = x_ref[...]
    hm = hmod(x)
    he = jnp.concatenate([hmod(xp_ref[...]), hm, hmod(xn_ref[...])], axis=0)
    y_attn = lax.dot_general(yt_ref[...], wb_ref[0], (((0,), (0,)), ((), ())), preferred_element_type=F32)
    z_a = jnp.dot(hm, w_ref[:, MIX_A:MIX_BG], preferred_element_type=F32)
    z_p = jnp.dot(he, w_ref[:, MIX_P:MIX_G], preferred_element_type=F32)
    z_bg = jnp.dot(hm, w_ref[:, MIX_BG:MIX_P], preferred_element_type=F32)
    for n in range(tm // CHUNK):
        rows = slice(n * CHUNK, (n + 1) * CHUNK)
        z = jax.nn.gelu(z_a[rows, :])
        u, v = z[:, :A_W], z[:, A_W:]
        vn = (_rms(v) * gv_ref[...]).astype(BF16)
        for g in range(A_GROUPS):
            cols = slice(g * A_GW, (g + 1) * A_GW)
            mixed = jnp.dot(ws_ref[g], vn[:, cols], preferred_element_type=F32) + bs_ref[:, cols]
            ya_ref[rows, cols] = (u[:, cols] * mixed).astype(BF16)
    z_g = jnp.dot(hm, w_ref[:, MIX_G:], preferred_element_type=F32)
    prod = z_p[:, :B_W] * z_p[:, B_W:]
    r = lax.broadcasted_iota(jnp.int32, (tm, B_W), 0)
    first = r < jnp.where(i == 0, 1, 0)
    last = r >= tm - jnp.where(i == nt - 1, 1, 0)
    pm1 = jnp.where(first, 0.0, pltpu.roll(prod, 1, 0)[HALO:HALO + tm])
    pp1 = jnp.where(last, 0.0, pltpu.roll(prod, tm + 2 * HALO - 1, 0)[HALO:HALO + tm])
    y_b = z_bg * (wc_ref[0:1, :] * pm1 + wc_ref[1:2, :] * prod[HALO:HALO + tm] + wc_ref[2:3, :] * pp1)
    p_a = jnp.dot(ya_ref[...], wb_ref[1], preferred_element_type=F32)
    gates = jax.nn.sigmoid(z_g + bg_ref[...])
    p_b = jnp.dot(y_b.astype(BF16), wb_ref[2], preferred_element_type=F32)
    merged = (gates[:, :D_MODEL] * y_attn + gates[:, D_MODEL:2 * D_MODEL] * p_a + gates[:, 2 * D_MODEL:] * p_b)
    out = jnp.dot(merged.astype(BF16), wo_ref[...], preferred_element_type=F32)
    o_ref[...] = x + mod_ref[:, 2 * D_MODEL:3 * D_MODEL] * out


def _mix(l, x, mod, mod_row, y_t, g, w_abg, ws, bs, gv, wc, bg, wb, wo, tm):
    b, s, _ = x.shape
    nt = s // tm
    params = (w_abg, ws, bs, gv, wc, bg, wb, wo)
    in_specs = [pl.BlockSpec((None, tm, D_MODEL), lambda bi, i: (bi, i, 0))] + _halo_specs(tm, s) + [
        _mod_spec(l, mod_row),
        _layer_spec(g, l),
        pl.BlockSpec((None, None, Q_W, tm), lambda bi, i: (bi, i, 0, 0)),
    ] + [_layer_spec(a, l) for a in params]
    return pl.pallas_call(
        functools.partial(_mix_kernel, nt=nt),
        grid=(b, nt),
        in_specs=in_specs,
        out_specs=pl.BlockSpec((None, tm, D_MODEL), lambda bi, i: (bi, i, 0)),
        out_shape=jax.ShapeDtypeStruct((b, s, D_MODEL), F32),
        scratch_shapes=[pltpu.VMEM((tm, A_W), BF16)],
        compiler_params=_cparams(2),
        name="mix",
    )(x, x, x, mod, g, y_t, *params)


def _ffn_kernel(x_ref, xp_ref, xn_ref, mod_ref, g_ref, wu_ref, wc_ref, wd_ref, gf_ref, o_ref, act_ref,
                *, nt, final, fc):
    tm = x_ref.shape[0]
    i = pl.program_id(1)
    sh, sc = mod_ref[:, 3 * D_MODEL:4 * D_MODEL], mod_ref[:, 4 * D_MODEL:5 * D_MODEL]

    def hmod(x):
        return ((_rms(x) * g_ref[...]) * (1.0 + sc) + sh).astype(BF16)

    x = x_ref[...]
    hm = hmod(x)
    he = jnp.concatenate([hmod(xp_ref[...]), hm, hmod(xn_ref[...])], axis=0)
    r = lax.broadcasted_iota(jnp.int32, (tm, fc), 0)
    first = r < jnp.where(i == 0, 1, 0)
    last = r >= tm - jnp.where(i == nt - 1, 1, 0)
    for c in range(0, D_FF, fc):
        a = jnp.dot(he, wu_ref[:, c:c + fc], preferred_element_type=F32)
        gt = jnp.dot(hm, wu_ref[:, D_FF + c:D_FF + c + fc], preferred_element_type=F32)
        am1 = jnp.where(first, 0.0, pltpu.roll(a, 1, 0)[HALO:HALO + tm])
        ap1 = jnp.where(last, 0.0, pltpu.roll(a, tm + 2 * HALO - 1, 0)[HALO:HALO + tm])
        conv = (wc_ref[0:1, c:c + fc] * am1 + wc_ref[1:2, c:c + fc] * a[HALO:HALO + tm]
                + wc_ref[2:3, c:c + fc] * ap1)
        act_ref[:, c:c + fc] = (jax.nn.silu(conv) * gt).astype(BF16)
    y = jnp.dot(act_ref[...], wd_ref[...], preferred_element_type=F32)
    out = x + mod_ref[:, 5 * D_MODEL:6 * D_MODEL] * y
    if final:
        out = _rms(out) * gf_ref[...]
    o_ref[...] = out


def _ffn(l, x, mod, mod_row, g, wu, wc, wd, gf, tm, final):
    b, s, _ = x.shape
    nt = s // tm
    in_specs = [pl.BlockSpec((None, tm, D_MODEL), lambda bi, i: (bi, i, 0))] + _halo_specs(tm, s) + [
        _mod_spec(l, mod_row),
        _layer_spec(g, l), _layer_spec(wu, l), _layer_spec(wc, l), _layer_spec(wd, l),
        pl.BlockSpec(gf.shape, lambda bi, i: (0, 0)),
    ]
    return pl.pallas_call(
        functools.partial(_ffn_kernel, nt=nt, final=final, fc=256),
        grid=(b, nt),
        in_specs=in_specs,
        out_specs=pl.BlockSpec((None, tm, D_MODEL), lambda bi, i: (bi, i, 0)),
        out_shape=jax.ShapeDtypeStruct((b, s, D_MODEL), F32),
        scratch_shapes=[pltpu.VMEM((tm, D_FF), BF16)],
        compiler_params=_cparams(2),
        name="ffn",
    )(x, x, x, mod, g, wu, wc, wd, gf)


def _rope_tables(n):
    rows = n // GRID_W
    row = jnp.broadcast_to(jnp.arange(rows, dtype=F32)[:, None], (rows, GRID_W)).reshape(n)
    col = jnp.broadcast_to(jnp.arange(GRID_W, dtype=F32)[None, :], (rows, GRID_W)).reshape(n)
    half = HEAD_DIM // 2
    inv = ROPE_THETA ** (-jnp.arange(0, half, 2, dtype=F32) / half)
    ang = jnp.concatenate([row[:, None] * inv, col[:, None] * inv], axis=-1)
    cos, sin = jnp.cos(ang), jnp.sin(ang)
    zero = jnp.zeros_like(sin)
    reps = LANES // HEAD_DIM
    cos_t = jnp.tile(jnp.concatenate([cos, cos], axis=-1), (1, reps))
    sin_hi = jnp.tile(jnp.concatenate([zero, sin], axis=-1), (1, reps))
    sin_lo = jnp.tile(jnp.concatenate([-sin, zero], axis=-1), (1, reps))
    return cos_t, sin_hi, sin_lo


def kernel(x, c, ctx, c_ctx, w_mod, b_mod, g_mix, w_in, b_gate, sink, w_spatial, b_spatial, g_v, w_sconv,
           w_branch, w_out, g_ffn, w_up, w_fconv, w_down, g_final):
    b, n, _ = x.shape
    m = ctx.shape[1]
    assert b + 1 <= MOD_ROWS and n % 512 == 0 and m % CHUNK == 0

    c_all = jnp.zeros((MOD_ROWS, D_MODEL), F32).at[:b].set(c).at[CTX_ROW].set(c_ctx)
    mod = _modulation(c_all, w_mod, b_mod).reshape(DEPTH, MOD_ROWS, 1, 6 * D_MODEL)
    rope_tabs = _rope_tables(n)

    row = lambda a: a.reshape(DEPTH, 1, -1)
    g_mix_r = row(g_mix)
    w_in_b = w_in.astype(BF16)
    bs_full = jnp.repeat(jnp.transpose(b_spatial, (0, 2, 1)), A_GW, axis=2)
    mix_w = (g_mix_r, w_in_b, w_spatial.astype(BF16), bs_full, row(g_v), w_sconv,
             row(b_gate), w_branch.astype(BF16), w_out.astype(BF16))
    ffn_w = (row(g_ffn), w_up.astype(BF16), w_fconv, w_down.astype(BF16), g_final.reshape(1, D_MODEL))

    lat_row = lambda bi: bi
    ctx_row = lambda bi: CTX_ROW
    tq, tmix, tffn = 1024, 512, 1024
    qkv_sections = lambda t: ((OFF_Q, Q_W, Q_W, QK_SCALE, t), (OFF_K, KV_W, KV_W, 1.0, 0),
                              (OFF_V, KV_W, 0, 1.0, BLK))

    xc = ctx
    for l in range(DEPTH):
        last = l == DEPTH - 1
        q_t, k, v_t = _inproj(l, x, mod, lat_row, g_mix_r, w_in_b, qkv_sections(tq), tq, rope_tabs)
        ctx_out = _inproj(l, xc, mod, ctx_row, g_mix_r, w_in_b, qkv_sections(m)[1 if last else 0:], m)
        kc, vc_t = ctx_out[-2:]
        y_t = _attention(l, q_t, kc, vc_t, sink, tmix, k=k, v_t=v_t)
        x = _mix(l, x, mod, lat_row, y_t, *mix_w, tm=tmix)
        x = _ffn(l, x, mod, lat_row, *ffn_w, tm=tffn, final=last)
        if not last:
            yc_t = _attention(l, ctx_out[0], kc, vc_t, sink, m)
            xc = _mix(l, xc, mod, ctx_row, yc_t, *mix_w, tm=m)
            xc = _ffn(l, xc, mod, ctx_row, *ffn_w, tm=m, final=False)
    return x
```

```python
import functools

import jax
import jax.numpy as jnp
from jax import lax
from jax.experimental import pallas as pl
from jax.experimental.pallas import tpu as pltpu

F32 = jnp.float32
BF16 = jnp.bfloat16

D_MODEL = 1024
DEPTH = 2
GRID_W = 64
N_HEADS = 16
N_KV_HEADS = 4
HEAD_DIM = 64
ROT_HALF = HEAD_DIM // 2
Q_W = N_HEADS * HEAD_DIM
KV_W = N_KV_HEADS * HEAD_DIM
BLK = 128
ROPE_THETA = 10000.0
CHUNK = 128
A_W = D_MODEL
A_GROUPS = 8
A_GW = A_W // A_GROUPS
B_W = D_MODEL
N_BRANCH = 3
D_FF = 2816
EPS = 1e-6
NEG = -1e30
OFF_Q = 0
OFF_K = OFF_Q + Q_W
OFF_V = OFF_K + KV_W
OFF_A = OFF_V + KV_W
OFF_B = OFF_A + 2 * A_W
OFF_G = OFF_B + 3 * B_W
IN_W = OFF_G + N_BRANCH * D_MODEL
MIX_A, MIX_BG, MIX_P, MIX_G = OFF_A, OFF_B, OFF_B + B_W, OFF_G

LOG2E = 1.4426950408889634
QK_SCALE = HEAD_DIM ** -0.5 * LOG2E

LANES = 128
HALO = 16
VMEM_LIMIT = 56 * 1024 * 1024
MOD_ROWS = 8
CTX_ROW = 4


def _cparams(n_axes):
    return pltpu.CompilerParams(
        dimension_semantics=("arbitrary",) * n_axes, vmem_limit_bytes=VMEM_LIMIT)


def _rms(x):
    return x * lax.rsqrt(jnp.mean(x * x, axis=-1, keepdims=True) + EPS)


def _layer_spec(a, l):
    return pl.BlockSpec((None,) + a.shape[1:], lambda bi, i: (l,) + (0,) * (a.ndim - 1),
                        pipeline_mode=pl.Buffered(1))


def _mod_spec(l, mod_row):
    return pl.BlockSpec((None, None, 1, 6 * D_MODEL), lambda bi, i: (l, mod_row(bi), 0, 0))


def _halo_specs(tm, s):
    per_tile, n_halo = tm // HALO, s // HALO
    return [
        pl.BlockSpec((None, HALO, D_MODEL), lambda bi, i: (bi, jnp.maximum(i * per_tile - 1, 0), 0)),
        pl.BlockSpec((None, HALO, D_MODEL), lambda bi, i: (bi, jnp.minimum((i + 1) * per_tile, n_halo - 1), 0)),
    ]


def _mod_kernel(c_ref, w_ref, b_ref, o_ref):
    a = jax.nn.silu(c_ref[...]).astype(BF16)
    o_ref[...] = jnp.dot(a, w_ref[...].astype(BF16), preferred_element_type=F32) + b_ref[...]


def _modulation(c_all, w_mod, b_mod):
    tn = 1536
    n_out = 6 * D_MODEL
    return pl.pallas_call(
        _mod_kernel,
        grid=(DEPTH, n_out // tn),
        in_specs=[
            pl.BlockSpec((MOD_ROWS, D_MODEL), lambda l, j: (0, 0)),
            pl.BlockSpec((None, D_MODEL, tn), lambda l, j: (l, 0, j)),
            pl.BlockSpec((None, 1, tn), lambda l, j: (l, 0, j)),
        ],
        out_specs=pl.BlockSpec((None, MOD_ROWS, tn), lambda l, j: (l, 0, j)),
        out_shape=jax.ShapeDtypeStruct((DEPTH, MOD_ROWS, n_out), F32),
        compiler_params=_cparams(2),
        name="modulation",
    )(c_all, w_mod, b_mod.reshape(DEPTH, 1, n_out))


def _inproj_kernel(*refs, sections, rope, tn):
    if rope:
        x_ref, mod_ref, g_ref, cos_ref, sin_ref, w_ref = refs[:6]
        out_refs = refs[6:]
    else:
        x_ref, mod_ref, g_ref, w_ref = refs[:4]
        out_refs = refs[4:]
    h = _rms(x_ref[...]) * g_ref[...]
    h = h * (1.0 + mod_ref[:, D_MODEL:2 * D_MODEL]) + mod_ref[:, 0:D_MODEL]
    hb = h.astype(BF16)
    for (start, width, rope_w, scale, tile), o_ref in zip(sections, out_refs):
        for c0 in range(0, width, tn):
            w = min(tn, width - c0)
            acc = jnp.dot(hb, w_ref[:, start + c0:start + c0 + w], preferred_element_type=F32)
            for cc in range(0, w, LANES):
                t = acc[:, cc:cc + LANES]
                if rope and c0 + cc < rope_w:
                    t = t * cos_ref[...] + pltpu.roll(t, HEAD_DIM, 1) * sin_ref[...]
                if scale != 1.0:
                    t = t * scale
                if tile:
                    for jj in range(t.shape[0] // tile):
                        o_ref[jj, c0 + cc:c0 + cc + LANES, :] = jnp.transpose(
                            t[jj * tile:(jj + 1) * tile, :]).astype(BF16)
                else:
                    o_ref[:, c0 + cc:c0 + cc + LANES] = t.astype(BF16)


def _inproj(l, x, mod, mod_row, g, w, sections, tm, rope_tabs=None):
    b, s, _ = x.shape
    nt = s // tm
    rope = rope_tabs is not None
    in_specs = [
        pl.BlockSpec((None, tm, D_MODEL), lambda bi, i: (bi, i, 0)),
        _mod_spec(l, mod_row),
        _layer_spec(g, l),
    ]
    args = [x, mod, g]
    if rope:
        in_specs += [pl.BlockSpec((tm, LANES), lambda bi, i: (i, 0))] * 2
        args += list(rope_tabs)
    in_specs.append(_layer_spec(w, l))
    args.append(w)
    outs = pl.pallas_call(
        functools.partial(_inproj_kernel, sections=tuple(sections), rope=rope, tn=512),
        grid=(b, nt),
        in_specs=in_specs,
        out_specs=[pl.BlockSpec((None, tm // sec[4], sec[1], sec[4]), lambda bi, i: (bi, i, 0, 0)) if sec[4]
                   else pl.BlockSpec((None, tm, sec[1]), lambda bi, i: (bi, i, 0)) for sec in sections],
        out_shape=[jax.ShapeDtypeStruct((b, s // sec[4], sec[1], sec[4]) if sec[4] else (b, s, sec[1]), BF16)
                   for sec in sections],
        compiler_params=_cparams(2),
        name="inproj",
    )(*args)
    return outs


def _attn_kernel(*refs, l, local, nb):
    if local:
        qt_ref, kp_ref, km_ref, kn_ref, vp_ref, vm_ref, vn_ref, kc_ref, vc_ref, sink_ref, yt_ref = refs
    else:
        qt_ref, kc_ref, vc_ref, sink_ref, yt_ref = refs
    nsub = qt_ref.shape[1] // BLK
    ty = yt_ref.shape[2]
    grp = N_HEADS // N_KV_HEADS
    width = grp * BLK
    i = pl.program_id(1)
    if local:
        kj = lax.broadcasted_iota(jnp.int32, (BLK, width), 0)
        qi = lax.broadcasted_iota(jnp.int32, (BLK, width), 1) & (BLK - 1)
        band_prev, band_next = kj >= qi, kj <= qi
        edge_prev = kj >= qi + jnp.where(i > 0, 0, BLK)
        edge_next = kj <= qi - jnp.where(i < nb - 1, 0, BLK)
    zeros = jnp.zeros((ROT_HALF, width), BF16)

    def key_chunks(j, lanes):
        ks = [kc_ref[:, lanes]]
        if local:
            blk = lambda t: slice(t * BLK, (t + 1) * BLK)
            ks.append(kp_ref[:, lanes] if j == 0 else km_ref[blk(j - 1), lanes])
            ks.append(km_ref[blk(j), lanes])
            ks.append(kn_ref[:, lanes] if j == nsub - 1 else km_ref[blk(j + 1), lanes])
        return ks

    def value_chunks(j, rows):
        vs = [vc_ref[t, rows, :] for t in range(vc_ref.shape[0])]
        if local:
            vs.append(vp_ref[rows, :] if j == 0 else vm_ref[j - 1, rows, :])
            vs.append(vm_ref[j, rows, :])
            vs.append(vn_ref[rows, :] if j == nsub - 1 else vm_ref[j + 1, rows, :])
        return vs

    def scores(j, kh):
        pair, half = kh // 2, kh % 2
        cols = slice(j * BLK, (j + 1) * BLK)
        def piece(r, hi):
            h = kh * grp + r
            row0 = (h // 2) * LANES + hi * HEAD_DIM + (h % 2) * ROT_HALF
            return qt_ref[row0:row0 + ROT_HALF, cols]
        q_lo = jnp.concatenate([piece(r, 0) for r in range(grp)], axis=1)
        q_hi = jnp.concatenate([piece(r, 1) for r in range(grp)], axis=1)
        q_pad = jnp.concatenate([q_lo, zeros, q_hi, zeros] if half == 0 else [zeros, q_lo, zeros, q_hi], axis=0)
        ss = [jnp.dot(kk, q_pad, preferred_element_type=F32)
              for kk in key_chunks(j, slice(pair * LANES, (pair + 1) * LANES))]
        if local:
            ss[1] = jnp.where(edge_prev if j == 0 else band_prev, ss[1], NEG)
            ss[3] = jnp.where(edge_next if j == nsub - 1 else band_next, ss[3], NEG)
        m = jnp.max(ss[0], axis=0, keepdims=True)
        for s_x in ss[1:]:
            m = jnp.maximum(m, jnp.max(s_x, axis=0, keepdims=True))
        return ss, m

    def finish(j, kh, scored):
        ss, m = scored
        sink = jnp.concatenate(
            [jnp.full((1, BLK), sink_ref[l, kh * grp + r] * LOG2E, F32) for r in range(grp)], axis=1)
        m = jnp.maximum(m, sink)
        p_t = jnp.concatenate([jnp.exp2((s_x - m).astype(BF16)) for s_x in ss], axis=0)
        v_t = jnp.concatenate(value_chunks(j, slice(kh * HEAD_DIM, (kh + 1) * HEAD_DIM)), axis=1)
        v_ext = jnp.concatenate([v_t, jnp.ones((HALO, v_t.shape[1]), BF16)], axis=0)
        o_ext = jnp.dot(v_ext, p_t, preferred_element_type=F32)
        den = o_ext[HEAD_DIM:HEAD_DIM + 1, :] + jnp.exp2(sink - m)
        o_t = o_ext[:HEAD_DIM, :] / den
        for r in range(grp):
            h = kh * grp + r
            t0 = j * BLK
            yt_ref[t0 // ty, h * HEAD_DIM:(h + 1) * HEAD_DIM, t0 % ty:t0 % ty + BLK] = (
                o_t[:, r * BLK:(r + 1) * BLK].astype(BF16))

    items = [(j, kh) for j in range(nsub) for kh in range(N_KV_HEADS)]
    ss_next = scores(*items[0])
    for n, (j, kh) in enumerate(items):
        ss = ss_next
        if n + 1 < len(items):
            ss_next = scores(*items[n + 1])
        finish(j, kh, ss)


def _attention(l, q_t, kc, vc_t, sink, ty, k=None, v_t=None):
    b, nt, _, tq = q_t.shape
    s = nt * tq
    m = kc.shape[1]
    local = k is not None
    nsub, nblk = tq // BLK, s // BLK
    prev_b = lambda i: jnp.maximum(i * nsub - 1, 0)
    next_b = lambda i: jnp.minimum((i + 1) * nsub, nblk - 1)
    in_specs = [pl.BlockSpec((None, None, Q_W, tq), lambda bi, i: (bi, i, 0, 0))]
    args = [q_t]
    if local:
        in_specs += [
            pl.BlockSpec((None, BLK, KV_W), lambda bi, i: (bi, prev_b(i), 0)),
            pl.BlockSpec((None, tq, KV_W), lambda bi, i: (bi, i, 0)),
            pl.BlockSpec((None, BLK, KV_W), lambda bi, i: (bi, next_b(i), 0)),
            pl.BlockSpec((None, None, KV_W, BLK), lambda bi, i: (bi, prev_b(i), 0, 0)),
            pl.BlockSpec((None, nsub, KV_W, BLK), lambda bi, i: (bi, i, 0, 0)),
            pl.BlockSpec((None, None, KV_W, BLK), lambda bi, i: (bi, next_b(i), 0, 0)),
        ]
        args += [k, k, k, v_t, v_t, v_t]
    in_specs += [
        pl.BlockSpec((None, m, KV_W), lambda bi, i: (bi, 0, 0)),
        pl.BlockSpec((None, m // BLK, KV_W, BLK), lambda bi, i: (bi, 0, 0, 0)),
        pl.BlockSpec(memory_space=pltpu.SMEM),
    ]
    args += [kc, vc_t, sink]
    return pl.pallas_call(
        functools.partial(_attn_kernel, l=l, local=local, nb=nt),
        grid=(b, nt),
        in_specs=in_specs,
        out_specs=pl.BlockSpec((None, tq // ty, Q_W, ty), lambda bi, i: (bi, i, 0, 0)),
        out_shape=jax.ShapeDtypeStruct((b, s // ty, Q_W, ty), BF16),
        compiler_params=_cparams(2),
        name="attention",
    )(*args)


def _mix_kernel(x_ref, xp_ref, xn_ref, mod_ref, g_ref, yt_ref, w_ref, ws_ref, bs_ref, gv_ref, wc_ref,
                bg_ref, wb_ref, wo_ref, o_ref, ya_ref, *, nt):
    tm = x_ref.shape[0]
    i = pl.program_id(1)
    sh, sc = mod_ref[:, 0:D_MODEL], mod_ref[:, D_MODEL:2 * D_MODEL]

    def hmod(x):
        return ((_rms(x) * g_ref[...]) * (1.0 + sc) + sh).astype(BF16)

    x = x_ref[...]
    hm = hmod(x)
    he = jnp.concatenate([hmod(xp_ref[...]), hm, hmod(xn_ref[...])], axis=0)
    y_attn = lax.dot_general(yt_ref[...], wb_ref[0], (((0,), (0,)), ((), ())), preferred_element_type=F32)
    z_a = jnp.dot(hm, w_ref[:, MIX_A:MIX_BG], preferred_element_type=F32)
    z_p = jnp.dot(he, w_ref[:, MIX_P:MIX_G], preferred_element_type=F32)
    z_bg = jnp.dot(hm, w_ref[:, MIX_BG:MIX_P], preferred_element_type=F32)
    for n in range(tm // CHUNK):
        rows = slice(n * CHUNK, (n + 1) * CHUNK)
        z = jax.nn.gelu(z_a[rows, :])
        u, v = z[:, :A_W], z[:, A_W:]
        vn = (_rms(v) * gv_ref[...]).astype(BF16)
        for g in range(A_GROUPS):
            cols = slice(g * A_GW, (g + 1) * A_GW)
            mixed = jnp.dot(ws_ref[g], vn[:, cols], preferred_element_type=F32) + bs_ref[:, cols]
            ya_ref[rows, cols] = (u[:, cols] * mixed).astype(BF16)
    z_g = jnp.dot(hm, w_ref[:, MIX_G:], preferred_element_type=F32)
    prod = z_p[:, :B_W] * z_p[:, B_W:]
    r = lax.broadcasted_iota(jnp.int32, (tm, B_W), 0)
    first = r < jnp.where(i == 0, 1, 0)
    last = r >= tm - jnp.where(i == nt - 1, 1, 0)
    pm1 = jnp.where(first, 0.0, pltpu.roll(prod, 1, 0)[HALO:HALO + tm])
    pp1 = jnp.where(last, 0.0, pltpu.roll(prod, tm + 2 * HALO - 1, 0)[HALO:HALO + tm])
    y_b = z_bg * (wc_ref[0:1, :] * pm1 + wc_ref[1:2, :] * prod[HALO:HALO + tm] + wc_ref[2:3, :] * pp1)
    p_a = jnp.dot(ya_ref[...], wb_ref[1], preferred_element_type=F32)
    gates = jax.nn.sigmoid(z_g + bg_ref[...])
    p_b = jnp.dot(y_b.astype(BF16), wb_ref[2], preferred_element_type=F32)
    merged = (gates[:, :D_MODEL] * y_attn + gates[:, D_MODEL:2 * D_MODEL] * p_a + gates[:, 2 * D_MODEL:] * p_b)
    out = jnp.dot(merged.astype(BF16), wo_ref[...], preferred_element_type=F32)
    o_ref[...] = x + mod_ref[:, 2 * D_MODEL:3 * D_MODEL] * out


def _mix(l, x, mod, mod_row, y_t, g, w_abg, ws, bs, gv, wc, bg, wb, wo, tm):
    b, s, _ = x.shape
    nt = s // tm
    params = (w_abg, ws, bs, gv, wc, bg, wb, wo)
    in_specs = [pl.BlockSpec((None, tm, D_MODEL), lambda bi, i: (bi, i, 0))] + _halo_specs(tm, s) + [
        _mod_spec(l, mod_row),
        _layer_spec(g, l),
        pl.BlockSpec((None, None, Q_W, tm), lambda bi, i: (bi, i, 0, 0)),
    ] + [_layer_spec(a, l) for a in params]
    return pl.pallas_call(
        functools.partial(_mix_kernel, nt=nt),
        grid=(b, nt),
        in_specs=in_specs,
        out_specs=pl.BlockSpec((None, tm, D_MODEL), lambda bi, i: (bi, i, 0)),
        out_shape=jax.ShapeDtypeStruct((b, s, D_MODEL), F32),
        scratch_shapes=[pltpu.VMEM((tm, A_W), BF16)],
        compiler_params=_cparams(2),
        name="mix",
    )(x, x, x, mod, g, y_t, *params)


def _ffn_kernel(x_ref, xp_ref, xn_ref, mod_ref, g_ref, wu_ref, wc_ref, wd_ref, gf_ref, o_ref, act_ref,
                *, nt, final, fc):
    tm = x_ref.shape[0]
    i = pl.program_id(1)
    sh, sc = mod_ref[:, 3 * D_MODEL:4 * D_MODEL], mod_ref[:, 4 * D_MODEL:5 * D_MODEL]

    def hmod(x):
        return ((_rms(x) * g_ref[...]) * (1.0 + sc) + sh).astype(BF16)

    x = x_ref[...]
    hm = hmod(x)
    he = jnp.concatenate([hmod(xp_ref[...]), hm, hmod(xn_ref[...])], axis=0)
    r = lax.broadcasted_iota(jnp.int32, (tm, fc), 0)
    first = r < jnp.where(i == 0, 1, 0)
    last = r >= tm - jnp.where(i == nt - 1, 1, 0)
    for c in range(0, D_FF, fc):
        a = jnp.dot(he, wu_ref[:, c:c + fc], preferred_element_type=F32)
        gt = jnp.dot(hm, wu_ref[:, D_FF + c:D_FF + c + fc], preferred_element_type=F32)
        am1 = jnp.where(first, 0.0, pltpu.roll(a, 1, 0)[HALO:HALO + tm])
        ap1 = jnp.where(last, 0.0, pltpu.roll(a, tm + 2 * HALO - 1, 0)[HALO:HALO + tm])
        conv = (wc_ref[0:1, c:c + fc] * am1 + wc_ref[1:2, c:c + fc] * a[HALO:HALO + tm]
                + wc_ref[2:3, c:c + fc] * ap1)
        act_ref[:, c:c + fc] = (jax.nn.silu(conv) * gt).astype(BF16)
    y = jnp.dot(act_ref[...], wd_ref[...], preferred_element_type=F32)
    out = x + mod_ref[:, 5 * D_MODEL:6 * D_MODEL] * y
    if final:
        out = _rms(out) * gf_ref[...]
    o_ref[...] = out


def _ffn(l, x, mod, mod_row, g, wu, wc, wd, gf, tm, final):
    b, s, _ = x.shape
    nt = s // tm
    in_specs = [pl.BlockSpec((None, tm, D_MODEL), lambda bi, i: (bi, i, 0))] + _halo_specs(tm, s) + [
        _mod_spec(l, mod_row),
        _layer_spec(g, l), _layer_spec(wu, l), _layer_spec(wc, l), _layer_spec(wd, l),
        pl.BlockSpec(gf.shape, lambda bi, i: (0, 0)),
    ]
    return pl.pallas_call(
        functools.partial(_ffn_kernel, nt=nt, final=final, fc=256),
        grid=(b, nt),
        in_specs=in_specs,
        out_specs=pl.BlockSpec((None, tm, D_MODEL), lambda bi, i: (bi, i, 0)),
        out_shape=jax.ShapeDtypeStruct((b, s, D_MODEL), F32),
        scratch_shapes=[pltpu.VMEM((tm, D_FF), BF16)],
        compiler_params=_cparams(2),
        name="ffn",
    )(x, x, x, mod, g, wu, wc, wd, gf)


def _rope_tables(n):
    rows = n // GRID_W
    row = jnp.broadcast_to(jnp.arange(rows, dtype=F32)[:, None], (rows, GRID_W)).reshape(n)
    col = jnp.broadcast_to(jnp.arange(GRID_W, dtype=F32)[None, :], (rows, GRID_W)).reshape(n)
    half = HEAD_DIM // 2
    inv = ROPE_THETA ** (-jnp.arange(0, half, 2, dtype=F32) / half)
    ang = jnp.concatenate([row[:, None] * inv, col[:, None] * inv], axis=-1)
    cos, sin = jnp.cos(ang), jnp.sin(ang)
    cos_t = jnp.concatenate([cos, cos, cos, cos], axis=-1)
    sin_t = jnp.concatenate([-sin, -sin, sin, sin], axis=-1)
    return cos_t, sin_t


def _pair_interleave(w):
    lead = w.shape[:-1]
    w = w.reshape(lead + (-1, 2, 2, ROT_HALF))
    return jnp.swapaxes(w, -3, -2).reshape(lead + (-1,))


def kernel(x, c, ctx, c_ctx, w_mod, b_mod, g_mix, w_in, b_gate, sink, w_spatial, b_spatial, g_v, w_sconv,
           w_branch, w_out, g_ffn, w_up, w_fconv, w_down, g_final):
    b, n, _ = x.shape
    m = ctx.shape[1]
    assert b + 1 <= MOD_ROWS and n % 512 == 0 and m % CHUNK == 0

    c_all = jnp.zeros((MOD_ROWS, D_MODEL), F32).at[:b].set(c).at[CTX_ROW].set(c_ctx)
    mod = _modulation(c_all, w_mod, b_mod).reshape(DEPTH, MOD_ROWS, 1, 6 * D_MODEL)
    rope_tabs = _rope_tables(n)

    row = lambda a: a.reshape(DEPTH, 1, -1)
    g_mix_r = row(g_mix)
    w_in_b = w_in.astype(BF16)
    w_qkv_b = jnp.concatenate([_pair_interleave(w_in_b[..., OFF_Q:OFF_K]), _pair_interleave(w_in_b[..., OFF_K:OFF_V]),
                               w_in_b[..., OFF_V:OFF_A]], axis=-1)
    bs_full = jnp.repeat(jnp.transpose(b_spatial, (0, 2, 1)), A_GW, axis=2)
    mix_w = (g_mix_r, w_in_b, w_spatial.astype(BF16), bs_full, row(g_v), w_sconv,
             row(b_gate), w_branch.astype(BF16), w_out.astype(BF16))
    ffn_w = (row(g_ffn), w_up.astype(BF16), w_fconv, w_down.astype(BF16), g_final.reshape(1, D_MODEL))

    lat_row = lambda bi: bi
    ctx_row = lambda bi: CTX_ROW
    tq, tmix, tffn = 1024, 512, 1024
    qkv_sections = lambda t: ((OFF_Q, Q_W, Q_W, QK_SCALE, t), (OFF_K, KV_W, KV_W, 1.0, 0),
                              (OFF_V, KV_W, 0, 1.0, BLK))

    xc = ctx
    for l in range(DEPTH):
        last = l == DEPTH - 1
        q_t, k, v_t = _inproj(l, x, mod, lat_row, g_mix_r, w_qkv_b, qkv_sections(tq), tq, rope_tabs)
        ctx_out = _inproj(l, xc, mod, ctx_row, g_mix_r, w_qkv_b, qkv_sections(m)[1 if last else 0:], m)
        kc, vc_t = ctx_out[-2:]
        y_t = _attention(l, q_t, kc, vc_t, sink, tmix, k=k, v_t=v_t)
        x = _mix(l, x, mod, lat_row, y_t, *mix_w, tm=tmix)
        x = _ffn(l, x, mod, lat_row, *ffn_w, tm=tffn, final=last)
        if not last:
            yc_t = _attention(l, ctx_out[0], kc, vc_t, sink, m)
            xc = _mix(l, xc, mod, ctx_row, yc_t, *mix_w, tm=m)
            xc = _ffn(l, xc, mod, ctx_row, *ffn_w, tm=m, final=False)
    return x
```

```python
import functools

import jax
import jax.numpy as jnp
from jax import lax
from jax.experimental import pallas as pl
from jax.experimental.pallas import tpu as pltpu

F32 = jnp.float32
BF16 = jnp.bfloat16

D_MODEL = 1024
DEPTH = 2
GRID_W = 64
N_HEADS = 16
N_KV_HEADS = 4
HEAD_DIM = 64
ROT_HALF = HEAD_DIM // 2
Q_W = N_HEADS * HEAD_DIM
KV_W = N_KV_HEADS * HEAD_DIM
BLK = 128
ROPE_THETA = 10000.0
CHUNK = 128
A_W = D_MODEL
A_GROUPS = 8
A_GW = A_W // A_GROUPS
B_W = D_MODEL
N_BRANCH = 3
D_FF = 2816
EPS = 1e-6
NEG = -1e30
OFF_Q = 0
OFF_K = OFF_Q + Q_W
OFF_V = OFF_K + KV_W
OFF_A = OFF_V + KV_W
OFF_B = OFF_A + 2 * A_W
OFF_G = OFF_B + 3 * B_W
IN_W = OFF_G + N_BRANCH * D_MODEL
MIX_A, MIX_BG, MIX_P, MIX_G = OFF_A, OFF_B, OFF_B + B_W, OFF_G

LOG2E = 1.4426950408889634
QK_SCALE = HEAD_DIM ** -0.5 * LOG2E

LANES = 128
HALO = 16
VMEM_LIMIT = 56 * 1024 * 1024
MOD_ROWS = 8
CTX_ROW = 4
ATTN_HEADS_PER_ITEM = 2


def _cparams(n_axes):
    return pltpu.CompilerParams(
        dimension_semantics=("arbitrary",) * n_axes, vmem_limit_bytes=VMEM_LIMIT)


def _rms(x):
    return x * lax.rsqrt(jnp.mean(x * x, axis=-1, keepdims=True) + EPS)


def _layer_spec(a, l):
    return pl.BlockSpec((None,) + a.shape[1:], lambda bi, i: (l,) + (0,) * (a.ndim - 1),
                        pipeline_mode=pl.Buffered(1))


def _mod_spec(l, mod_row):
    return pl.BlockSpec((None, None, 1, 6 * D_MODEL), lambda bi, i: (l, mod_row(bi), 0, 0))


def _halo_specs(tm, s):
    per_tile, n_halo = tm // HALO, s // HALO
    return [
        pl.BlockSpec((None, HALO, D_MODEL), lambda bi, i: (bi, jnp.maximum(i * per_tile - 1, 0), 0)),
        pl.BlockSpec((None, HALO, D_MODEL), lambda bi, i: (bi, jnp.minimum((i + 1) * per_tile, n_halo - 1), 0)),
    ]


def _mod_kernel(c_ref, w_ref, b_ref, o_ref):
    a = jax.nn.silu(c_ref[...]).astype(BF16)
    o_ref[...] = jnp.dot(a, w_ref[...].astype(BF16), preferred_element_type=F32) + b_ref[...]


def _modulation(c_all, w_mod, b_mod):
    tn = 1536
    n_out = 6 * D_MODEL
    return pl.pallas_call(
        _mod_kernel,
        grid=(DEPTH, n_out // tn),
        in_specs=[
            pl.BlockSpec((MOD_ROWS, D_MODEL), lambda l, j: (0, 0)),
            pl.BlockSpec((None, D_MODEL, tn), lambda l, j: (l, 0, j)),
            pl.BlockSpec((None, 1, tn), lambda l, j: (l, 0, j)),
        ],
        out_specs=pl.BlockSpec((None, MOD_ROWS, tn), lambda l, j: (l, 0, j)),
        out_shape=jax.ShapeDtypeStruct((DEPTH, MOD_ROWS, n_out), F32),
        compiler_params=_cparams(2),
        name="modulation",
    )(c_all, w_mod, b_mod.reshape(DEPTH, 1, n_out))


def _inproj_kernel(*refs, sections, rope, tn):
    if rope:
        x_ref, mod_ref, g_ref, cos_ref, sin_ref, w_ref = refs[:6]
        out_refs = refs[6:]
    else:
        x_ref, mod_ref, g_ref, w_ref = refs[:4]
        out_refs = refs[4:]
    h = _rms(x_ref[...]) * g_ref[...]
    h = h * (1.0 + mod_ref[:, D_MODEL:2 * D_MODEL]) + mod_ref[:, 0:D_MODEL]
    hb = h.astype(BF16)
    for (start, width, rope_w, scale, tile), o_ref in zip(sections, out_refs):
        for c0 in range(0, width, tn):
            w = min(tn, width - c0)
            acc = jnp.dot(hb, w_ref[:, start + c0:start + c0 + w], preferred_element_type=F32)
            for cc in range(0, w, LANES):
                t = acc[:, cc:cc + LANES]
                if rope and c0 + cc < rope_w:
                    t = t * cos_ref[...] + pltpu.roll(t, HEAD_DIM, 1) * sin_ref[...]
                if scale != 1.0:
                    t = t * scale
                if tile:
                    for jj in range(t.shape[0] // tile):
                        o_ref[jj, c0 + cc:c0 + cc + LANES, :] = jnp.transpose(
                            t[jj * tile:(jj + 1) * tile, :]).astype(BF16)
                else:
                    o_ref[:, c0 + cc:c0 + cc + LANES] = t.astype(BF16)


def _inproj(l, x, mod, mod_row, g, w, sections, tm, rope_tabs=None):
    b, s, _ = x.shape
    nt = s // tm
    rope = rope_tabs is not None
    in_specs = [
        pl.BlockSpec((None, tm, D_MODEL), lambda bi, i: (bi, i, 0)),
        _mod_spec(l, mod_row),
        _layer_spec(g, l),
    ]
    args = [x, mod, g]
    if rope:
        in_specs += [pl.BlockSpec((tm, LANES), lambda bi, i: (i, 0))] * 2
        args += list(rope_tabs)
    in_specs.append(_layer_spec(w, l))
    args.append(w)
    outs = pl.pallas_call(
        functools.partial(_inproj_kernel, sections=tuple(sections), rope=rope, tn=512),
        grid=(b, nt),
        in_specs=in_specs,
        out_specs=[pl.BlockSpec((None, tm // sec[4], sec[1], sec[4]), lambda bi, i: (bi, i, 0, 0)) if sec[4]
                   else pl.BlockSpec((None, tm, sec[1]), lambda bi, i: (bi, i, 0)) for sec in sections],
        out_shape=[jax.ShapeDtypeStruct((b, s // sec[4], sec[1], sec[4]) if sec[4] else (b, s, sec[1]), BF16)
                   for sec in sections],
        compiler_params=_cparams(2),
        name="inproj",
    )(*args)
    return outs


def _attn_kernel(*refs, l, local, nb):
    if local:
        qt_ref, kp_ref, km_ref, kn_ref, vp_ref, vm_ref, vn_ref, kc_ref, vc_ref, sink_ref, yt_ref = refs
    else:
        qt_ref, kc_ref, vc_ref, sink_ref, yt_ref = refs
    nsub = qt_ref.shape[1] // BLK
    ty = yt_ref.shape[2]
    grp = N_HEADS // N_KV_HEADS
    gsub = ATTN_HEADS_PER_ITEM
    width = gsub * BLK
    i = pl.program_id(1)
    if local:
        kj = lax.broadcasted_iota(jnp.int32, (BLK, width), 0)
        qi = lax.broadcasted_iota(jnp.int32, (BLK, width), 1) & (BLK - 1)
        band_prev, band_next = kj >= qi, kj <= qi
        edge_prev = kj >= qi + jnp.where(i > 0, 0, BLK)
        edge_next = kj <= qi - jnp.where(i < nb - 1, 0, BLK)
    zeros = jnp.zeros((ROT_HALF, width), BF16)

    def key_chunks(j, lanes):
        ks = [kc_ref[:, lanes]]
        if local:
            blk = lambda t: slice(t * BLK, (t + 1) * BLK)
            ks.append(kp_ref[:, lanes] if j == 0 else km_ref[blk(j - 1), lanes])
            ks.append(km_ref[blk(j), lanes])
            ks.append(kn_ref[:, lanes] if j == nsub - 1 else km_ref[blk(j + 1), lanes])
        return ks

    def value_chunks(j, rows):
        vs = [vc_ref[t, rows, :] for t in range(vc_ref.shape[0])]
        if local:
            vs.append(vp_ref[rows, :] if j == 0 else vm_ref[j - 1, rows, :])
            vs.append(vm_ref[j, rows, :])
            vs.append(vn_ref[rows, :] if j == nsub - 1 else vm_ref[j + 1, rows, :])
        return vs

    def scores(j, kh, g0):
        pair, half = kh // 2, kh % 2
        cols = slice(j * BLK, (j + 1) * BLK)
        def piece(r, hi):
            h = kh * grp + g0 + r
            row0 = (h // 2) * LANES + hi * HEAD_DIM + (h % 2) * ROT_HALF
            return qt_ref[row0:row0 + ROT_HALF, cols]
        q_lo = jnp.concatenate([piece(r, 0) for r in range(gsub)], axis=1)
        q_hi = jnp.concatenate([piece(r, 1) for r in range(gsub)], axis=1)
        q_pad = jnp.concatenate([q_lo, zeros, q_hi, zeros] if half == 0 else [zeros, q_lo, zeros, q_hi], axis=0)
        ss = [jnp.dot(kk, q_pad, preferred_element_type=F32)
              for kk in key_chunks(j, slice(pair * LANES, (pair + 1) * LANES))]
        if local:
            ss[1] = jnp.where(edge_prev if j == 0 else band_prev, ss[1], NEG)
            ss[3] = jnp.where(edge_next if j == nsub - 1 else band_next, ss[3], NEG)
        m = jnp.max(ss[0], axis=0, keepdims=True)
        for s_x in ss[1:]:
            m = jnp.maximum(m, jnp.max(s_x, axis=0, keepdims=True))
        return ss, m

    def softmax(kh, g0, scored):
        ss, m = scored
        sink = jnp.concatenate(
            [jnp.full((1, BLK), sink_ref[l, kh * grp + g0 + r] * LOG2E, F32) for r in range(gsub)], axis=1)
        m = jnp.maximum(m, sink)
        p_t = jnp.concatenate([jnp.exp2((s_x - m).astype(BF16)) for s_x in ss], axis=0)
        return p_t, jnp.exp2(sink - m)

    def weighted_values(j, kh, g0, p_t, p_sink):
        v_t = jnp.concatenate(value_chunks(j, slice(kh * HEAD_DIM, (kh + 1) * HEAD_DIM)), axis=1)
        v_ext = jnp.concatenate([v_t, jnp.ones((HALO, v_t.shape[1]), BF16)], axis=0)
        o_ext = jnp.dot(v_ext, p_t, preferred_element_type=F32)
        o_t = o_ext[:HEAD_DIM, :] / (o_ext[HEAD_DIM:HEAD_DIM + 1, :] + p_sink)
        for r in range(gsub):
            h = kh * grp + g0 + r
            t0 = j * BLK
            yt_ref[t0 // ty, h * HEAD_DIM:(h + 1) * HEAD_DIM, t0 % ty:t0 % ty + BLK] = (
                o_t[:, r * BLK:(r + 1) * BLK].astype(BF16))

    items = [(j, kh, g0) for j in range(nsub) for kh in range(N_KV_HEADS) for g0 in range(0, grp, gsub)]
    n_items = len(items)
    scored = {0: scores(*items[0]), 1: scores(*items[1])}
    probs = {0: softmax(*items[0][1:], scored.pop(0))}
    for n, (j, kh, g0) in enumerate(items):
        if n + 2 < n_items:
            scored[n + 2] = scores(*items[n + 2])
        if n + 1 < n_items:
            probs[n + 1] = softmax(*items[n + 1][1:], scored.pop(n + 1))
        weighted_values(j, kh, g0, *probs.pop(n))


def _attention(l, q_t, kc, vc_t, sink, ty, k=None, v_t=None):
    b, nt, _, tq = q_t.shape
    s = nt * tq
    m = kc.shape[1]
    local = k is not None
    nsub, nblk = tq // BLK, s // BLK
    prev_b = lambda i: jnp.maximum(i * nsub - 1, 0)
    next_b = lambda i: jnp.minimum((i + 1) * nsub, nblk - 1)
    in_specs = [pl.BlockSpec((None, None, Q_W, tq), lambda bi, i: (bi, i, 0, 0))]
    args = [q_t]
    if local:
        in_specs += [
            pl.BlockSpec((None, BLK, KV_W), lambda bi, i: (bi, prev_b(i), 0)),
            pl.BlockSpec((None, tq, KV_W), lambda bi, i: (bi, i, 0)),
            pl.BlockSpec((None, BLK, KV_W), lambda bi, i: (bi, next_b(i), 0)),
            pl.BlockSpec((None, None, KV_W, BLK), lambda bi, i: (bi, prev_b(i), 0, 0)),
            pl.BlockSpec((None, nsub, KV_W, BLK), lambda bi, i: (bi, i, 0, 0)),
            pl.BlockSpec((None, None, KV_W, BLK), lambda bi, i: (bi, next_b(i), 0, 0)),
        ]
        args += [k, k, k, v_t, v_t, v_t]
    in_specs += [
        pl.BlockSpec((None, m, KV_W), lambda bi, i: (bi, 0, 0)),
        pl.BlockSpec((None, m // BLK, KV_W, BLK), lambda bi, i: (bi, 0, 0, 0)),
        pl.BlockSpec(memory_space=pltpu.SMEM),
    ]
    args += [kc, vc_t, sink]
    return pl.pallas_call(
        functools.partial(_attn_kernel, l=l, local=local, nb=nt),
        grid=(b, nt),
        in_specs=in_specs,
        out_specs=pl.BlockSpec((None, tq // ty, Q_W, ty), lambda bi, i: (bi, i, 0, 0)),
        out_shape=jax.ShapeDtypeStruct((b, s // ty, Q_W, ty), BF16),
        compiler_params=_cparams(2),
        name="attention",
    )(*args)


def _mix_kernel(x_ref, xp_ref, xn_ref, mod_ref, g_ref, yt_ref, w_ref, ws_ref, bs_ref, gv_ref, wc_ref,
                bg_ref, wb_ref, wo_ref, o_ref, ya_ref, *, nt):
    tm = x_ref.shape[0]
    i = pl.program_id(1)
    sh, sc = mod_ref[:, 0:D_MODEL], mod_ref[:, D_MODEL:2 * D_MODEL]

    def hmod(x):
        return ((_rms(x) * g_ref[...]) * (1.0 + sc) + sh).astype(BF16)

    x = x_ref[...]
    hm = hmod(x)
    he = jnp.concatenate([hmod(xp_ref[...]), hm, hmod(xn_ref[...])], axis=0)
    y_attn = lax.dot_general(yt_ref[...], wb_ref[0], (((0,), (0,)), ((), ())), preferred_element_type=F32)
    z_a = jnp.dot(hm, w_ref[:, MIX_A:MIX_BG], preferred_element_type=F32)
    z_p = jnp.dot(he, w_ref[:, MIX_P:MIX_G], preferred_element_type=F32)
    z_bg = jnp.dot(hm, w_ref[:, MIX_BG:MIX_P], preferred_element_type=F32)
    for n in range(tm // CHUNK):
        rows = slice(n * CHUNK, (n + 1) * CHUNK)
        z = jax.nn.gelu(z_a[rows, :])
        u, v = z[:, :A_W], z[:, A_W:]
        vn = (_rms(v) * gv_ref[...]).astype(BF16)
        for g in range(A_GROUPS):
            cols = slice(g * A_GW, (g + 1) * A_GW)
            mixed = jnp.dot(ws_ref[g], vn[:, cols], preferred_element_type=F32) + bs_ref[:, cols]
            ya_ref[rows, cols] = (u[:, cols] * mixed).astype(BF16)
    z_g = jnp.dot(hm, w_ref[:, MIX_G:], preferred_element_type=F32)
    prod = z_p[:, :B_W] * z_p[:, B_W:]
    r = lax.broadcasted_iota(jnp.int32, (tm, B_W), 0)
    first = r < jnp.where(i == 0, 1, 0)
    last = r >= tm - jnp.where(i == nt - 1, 1, 0)
    pm1 = jnp.where(first, 0.0, pltpu.roll(prod, 1, 0)[HALO:HALO + tm])
    pp1 = jnp.where(last, 0.0, pltpu.roll(prod, tm + 2 * HALO - 1, 0)[HALO:HALO + tm])
    y_b = z_bg * (wc_ref[0:1, :] * pm1 + wc_ref[1:2, :] * prod[HALO:HALO + tm] + wc_ref[2:3, :] * pp1)
    p_a = jnp.dot(ya_ref[...], wb_ref[1], preferred_element_type=F32)
    gates = jax.nn.sigmoid(z_g + bg_ref[...])
    p_b = jnp.dot(y_b.astype(BF16), wb_ref[2], preferred_element_type=F32)
    merged = (gates[:, :D_MODEL] * y_attn + gates[:, D_MODEL:2 * D_MODEL] * p_a + gates[:, 2 * D_MODEL:] * p_b)
    out = jnp.dot(merged.astype(BF16), wo_ref[...], preferred_element_type=F32)
    o_ref[...] = x + mod_ref[:, 2 * D_MODEL:3 * D_MODEL] * out


def _mix(l, x, mod, mod_row, y_t, g, w_abg, ws, bs, gv, wc, bg, wb, wo, tm):
    b, s, _ = x.shape
    nt = s // tm
    params = (w_abg, ws, bs, gv, wc, bg, wb, wo)
    in_specs = [pl.BlockSpec((None, tm, D_MODEL), lambda bi, i: (bi, i, 0))] + _halo_specs(tm, s) + [
        _mod_spec(l, mod_row),
        _layer_spec(g, l),
        pl.BlockSpec((None, None, Q_W, tm), lambda bi, i: (bi, i, 0, 0)),
    ] + [_layer_spec(a, l) for a in params]
    return pl.pallas_call(
        functools.partial(_mix_kernel, nt=nt),
        grid=(b, nt),
        in_specs=in_specs,
        out_specs=pl.BlockSpec((None, tm, D_MODEL), lambda bi, i: (bi, i, 0)),
        out_shape=jax.ShapeDtypeStruct((b, s, D_MODEL), F32),
        scratch_shapes=[pltpu.VMEM((tm, A_W), BF16)],
        compiler_params=_cparams(2),
        name="mix",
    )(x, x, x, mod, g, y_t, *params)


def _ffn_kernel(x_ref, xp_ref, xn_ref, mod_ref, g_ref, wu_ref, wc_ref, wd_ref, gf_ref, o_ref, act_ref,
                *, nt, final, fc):
    tm = x_ref.shape[0]
    i = pl.program_id(1)
    sh, sc = mod_ref[:, 3 * D_MODEL:4 * D_MODEL], mod_ref[:, 4 * D_MODEL:5 * D_MODEL]

    def hmod(x):
        return ((_rms(x) * g_ref[...]) * (1.0 + sc) + sh).astype(BF16)

    x = x_ref[...]
    hm = hmod(x)
    he = jnp.concatenate([hmod(xp_ref[...]), hm, hmod(xn_ref[...])], axis=0)
    r = lax.broadcasted_iota(jnp.int32, (tm, fc), 0)
    first = r < jnp.where(i == 0, 1, 0)
    last = r >= tm - jnp.where(i == nt - 1, 1, 0)
    for c in range(0, D_FF, fc):
        a = jnp.dot(he, wu_ref[:, c:c + fc], preferred_element_type=F32)
        gt = jnp.dot(hm, wu_ref[:, D_FF + c:D_FF + c + fc], preferred_element_type=F32)
        am1 = jnp.where(first, 0.0, pltpu.roll(a, 1, 0)[HALO:HALO + tm])
        ap1 = jnp.where(last, 0.0, pltpu.roll(a, tm + 2 * HALO - 1, 0)[HALO:HALO + tm])
        conv = (wc_ref[0:1, c:c + fc] * am1 + wc_ref[1:2, c:c + fc] * a[HALO:HALO + tm]
                + wc_ref[2:3, c:c + fc] * ap1)
        act_ref[:, c:c + fc] = (jax.nn.silu(conv) * gt).astype(BF16)
    y = jnp.dot(act_ref[...], wd_ref[...], preferred_element_type=F32)
    out = x + mod_ref[:, 5 * D_MODEL:6 * D_MODEL] * y
    if final:
        out = _rms(out) * gf_ref[...]
    o_ref[...] = out


def _ffn(l, x, mod, mod_row, g, wu, wc, wd, gf, tm, final):
    b, s, _ = x.shape
    nt = s // tm
    in_specs = [pl.BlockSpec((None, tm, D_MODEL), lambda bi, i: (bi, i, 0))] + _halo_specs(tm, s) + [
        _mod_spec(l, mod_row),
        _layer_spec(g, l), _layer_spec(wu, l), _layer_spec(wc, l), _layer_spec(wd, l),
        pl.BlockSpec(gf.shape, lambda bi, i: (0, 0)),
    ]
    return pl.pallas_call(
        functools.partial(_ffn_kernel, nt=nt, final=final, fc=256),
        grid=(b, nt),
        in_specs=in_specs,
        out_specs=pl.BlockSpec((None, tm, D_MODEL), lambda bi, i: (bi, i, 0)),
        out_shape=jax.ShapeDtypeStruct((b, s, D_MODEL), F32),
        scratch_shapes=[pltpu.VMEM((tm, D_FF), BF16)],
        compiler_params=_cparams(2),
        name="ffn",
    )(x, x, x, mod, g, wu, wc, wd, gf)


def _rope_tables(n):
    rows = n // GRID_W
    row = jnp.broadcast_to(jnp.arange(rows, dtype=F32)[:, None], (rows, GRID_W)).reshape(n)
    col = jnp.broadcast_to(jnp.arange(GRID_W, dtype=F32)[None, :], (rows, GRID_W)).reshape(n)
    half = HEAD_DIM // 2
    inv = ROPE_THETA ** (-jnp.arange(0, half, 2, dtype=F32) / half)
    ang = jnp.concatenate([row[:, None] * inv, col[:, None] * inv], axis=-1)
    cos, sin = jnp.cos(ang), jnp.sin(ang)
    cos_t = jnp.concatenate([cos, cos, cos, cos], axis=-1)
    sin_t = jnp.concatenate([-sin, -sin, sin, sin], axis=-1)
    return cos_t, sin_t


def _pair_interleave(w):
    lead = w.shape[:-1]
    w = w.reshape(lead + (-1, 2, 2, ROT_HALF))
    return jnp.swapaxes(w, -3, -2).reshape(lead + (-1,))


def kernel(x, c, ctx, c_ctx, w_mod, b_mod, g_mix, w_in, b_gate, sink, w_spatial, b_spatial, g_v, w_sconv,
           w_branch, w_out, g_ffn, w_up, w_fconv, w_down, g_final):
    b, n, _ = x.shape
    m = ctx.shape[1]
    assert b + 1 <= MOD_ROWS and n % 512 == 0 and m % CHUNK == 0

    c_all = jnp.zeros((MOD_ROWS, D_MODEL), F32).at[:b].set(c).at[CTX_ROW].set(c_ctx)
    mod = _modulation(c_all, w_mod, b_mod).reshape(DEPTH, MOD_ROWS, 1, 6 * D_MODEL)
    rope_tabs = _rope_tables(n)

    row = lambda a: a.reshape(DEPTH, 1, -1)
    g_mix_r = row(g_mix)
    w_in_b = w_in.astype(BF16)
    w_qkv_b = jnp.concatenate([_pair_interleave(w_in_b[..., OFF_Q:OFF_K]), _pair_interleave(w_in_b[..., OFF_K:OFF_V]),
                               w_in_b[..., OFF_V:OFF_A]], axis=-1)
    bs_full = jnp.repeat(jnp.transpose(b_spatial, (0, 2, 1)), A_GW, axis=2)
    mix_w = (g_mix_r, w_in_b, w_spatial.astype(BF16), bs_full, row(g_v), w_sconv,
             row(b_gate), w_branch.astype(BF16), w_out.astype(BF16))
    ffn_w = (row(g_ffn), w_up.astype(BF16), w_fconv, w_down.astype(BF16), g_final.reshape(1, D_MODEL))

    lat_row = lambda bi: bi
    ctx_row = lambda bi: CTX_ROW
    tq, tmix, tffn = 1024, 512, 1024
    qkv_sections = lambda t: ((OFF_Q, Q_W, Q_W, QK_SCALE, t), (OFF_K, KV_W, KV_W, 1.0, 0),
                              (OFF_V, KV_W, 0, 1.0, BLK))

    xc = ctx
    for l in range(DEPTH):
        last = l == DEPTH - 1
        q_t, k, v_t = _inproj(l, x, mod, lat_row, g_mix_r, w_qkv_b, qkv_sections(tq), tq, rope_tabs)
        ctx_out = _inproj(l, xc, mod, ctx_row, g_mix_r, w_qkv_b, qkv_sections(m)[1 if last else 0:], m)
        kc, vc_t = ctx_out[-2:]
        y_t = _attention(l, q_t, kc, vc_t, sink, tmix, k=k, v_t=v_t)
        x = _mix(l, x, mod, lat_row, y_t, *mix_w, tm=tmix)
        x = _ffn(l, x, mod, lat_row, *ffn_w, tm=tffn, final=last)
        if not last:
            yc_t = _attention(l, ctx_out[0], kc, vc_t, sink, m)
            xc = _mix(l, xc, mod, ctx_row, yc_t, *mix_w, tm=m)
            xc = _ffn(l, xc, mod, ctx_row, *ffn_w, tm=m, final=False)
    return x
```

```python
import functools

import jax
import jax.numpy as jnp
from jax import lax
from jax.experimental import pallas as pl
from jax.experimental.pallas import tpu as pltpu

F32 = jnp.float32
BF16 = jnp.bfloat16

D_MODEL = 1024
DEPTH = 2
GRID_W = 64
N_HEADS = 16
N_KV_HEADS = 4
HEAD_DIM = 64
ROT_HALF = HEAD_DIM // 2
Q_W = N_HEADS * HEAD_DIM
KV_W = N_KV_HEADS * HEAD_DIM
BLK = 128
ROPE_THETA = 10000.0
CHUNK = 128
A_W = D_MODEL
A_GROUPS = 8
A_GW = A_W // A_GROUPS
B_W = D_MODEL
N_BRANCH = 3
D_FF = 2816
EPS = 1e-6
NEG = -1e30
OFF_Q = 0
OFF_K = OFF_Q + Q_W
OFF_V = OFF_K + KV_W
OFF_A = OFF_V + KV_W
OFF_B = OFF_A + 2 * A_W
OFF_G = OFF_B + 3 * B_W
IN_W = OFF_G + N_BRANCH * D_MODEL
MIX_A, MIX_BG, MIX_P, MIX_G = OFF_A, OFF_B, OFF_B + B_W, OFF_G

LOG2E = 1.4426950408889634
QK_SCALE = HEAD_DIM ** -0.5 * LOG2E

LANES = 128
HALO = 16
VMEM_LIMIT = 56 * 1024 * 1024
MOD_ROWS = 8
CTX_ROW = 4
ATTN_HEADS_PER_ITEM = 2
ATTN_SCORE_LEAD, ATTN_SOFTMAX_LEAD = 3, 1


def _cparams(n_axes):
    return pltpu.CompilerParams(
        dimension_semantics=("arbitrary",) * n_axes, vmem_limit_bytes=VMEM_LIMIT)


def _rms(x):
    return x * lax.rsqrt(jnp.mean(x * x, axis=-1, keepdims=True) + EPS)


def _layer_spec(a, l):
    return pl.BlockSpec((None,) + a.shape[1:], lambda bi, i: (l,) + (0,) * (a.ndim - 1),
                        pipeline_mode=pl.Buffered(1))


def _mod_spec(l, mod_row):
    return pl.BlockSpec((None, None, 1, 6 * D_MODEL), lambda bi, i: (l, mod_row(bi), 0, 0))


def _halo_specs(tm, s):
    per_tile, n_halo = tm // HALO, s // HALO
    return [
        pl.BlockSpec((None, HALO, D_MODEL), lambda bi, i: (bi, jnp.maximum(i * per_tile - 1, 0), 0)),
        pl.BlockSpec((None, HALO, D_MODEL), lambda bi, i: (bi, jnp.minimum((i + 1) * per_tile, n_halo - 1), 0)),
    ]


def _mod_kernel(c_ref, w_ref, b_ref, o_ref):
    a = jax.nn.silu(c_ref[...]).astype(BF16)
    o_ref[...] = jnp.dot(a, w_ref[...].astype(BF16), preferred_element_type=F32) + b_ref[...]


def _modulation(c_all, w_mod, b_mod):
    tn = 1536
    n_out = 6 * D_MODEL
    return pl.pallas_call(
        _mod_kernel,
        grid=(DEPTH, n_out // tn),
        in_specs=[
            pl.BlockSpec((MOD_ROWS, D_MODEL), lambda l, j: (0, 0)),
            pl.BlockSpec((None, D_MODEL, tn), lambda l, j: (l, 0, j)),
            pl.BlockSpec((None, 1, tn), lambda l, j: (l, 0, j)),
        ],
        out_specs=pl.BlockSpec((None, MOD_ROWS, tn), lambda l, j: (l, 0, j)),
        out_shape=jax.ShapeDtypeStruct((DEPTH, MOD_ROWS, n_out), F32),
        compiler_params=_cparams(2),
        name="modulation",
    )(c_all, w_mod, b_mod.reshape(DEPTH, 1, n_out))


def _inproj_kernel(*refs, sections, rope, tn):
    if rope:
        x_ref, mod_ref, g_ref, cos_ref, sin_ref, w_ref = refs[:6]
        out_refs = refs[6:]
    else:
        x_ref, mod_ref, g_ref, w_ref = refs[:4]
        out_refs = refs[4:]
    h = _rms(x_ref[...]) * g_ref[...]
    h = h * (1.0 + mod_ref[:, D_MODEL:2 * D_MODEL]) + mod_ref[:, 0:D_MODEL]
    hb = h.astype(BF16)
    for (start, width, rope_w, scale, tile), o_ref in zip(sections, out_refs):
        for c0 in range(0, width, tn):
            w = min(tn, width - c0)
            acc = jnp.dot(hb, w_ref[:, start + c0:start + c0 + w], preferred_element_type=F32)
            for cc in range(0, w, LANES):
                t = acc[:, cc:cc + LANES]
                if rope and c0 + cc < rope_w:
                    t = t * cos_ref[...] + pltpu.roll(t, HEAD_DIM, 1) * sin_ref[...]
                if scale != 1.0:
                    t = t * scale
                if tile:
                    for jj in range(t.shape[0] // tile):
                        o_ref[jj, c0 + cc:c0 + cc + LANES, :] = jnp.transpose(
                            t[jj * tile:(jj + 1) * tile, :]).astype(BF16)
                else:
                    o_ref[:, c0 + cc:c0 + cc + LANES] = t.astype(BF16)


def _inproj(l, x, mod, mod_row, g, w, sections, tm, rope_tabs=None):
    b, s, _ = x.shape
    nt = s // tm
    rope = rope_tabs is not None
    in_specs = [
        pl.BlockSpec((None, tm, D_MODEL), lambda bi, i: (bi, i, 0)),
        _mod_spec(l, mod_row),
        _layer_spec(g, l),
    ]
    args = [x, mod, g]
    if rope:
        in_specs += [pl.BlockSpec((tm, LANES), lambda bi, i: (i, 0))] * 2
        args += list(rope_tabs)
    in_specs.append(_layer_spec(w, l))
    args.append(w)
    outs = pl.pallas_call(
        functools.partial(_inproj_kernel, sections=tuple(sections), rope=rope, tn=512),
        grid=(b, nt),
        in_specs=in_specs,
        out_specs=[pl.BlockSpec((None, tm // sec[4], sec[1], sec[4]), lambda bi, i: (bi, i, 0, 0)) if sec[4]
                   else pl.BlockSpec((None, tm, sec[1]), lambda bi, i: (bi, i, 0)) for sec in sections],
        out_shape=[jax.ShapeDtypeStruct((b, s // sec[4], sec[1], sec[4]) if sec[4] else (b, s, sec[1]), BF16)
                   for sec in sections],
        compiler_params=_cparams(2),
        name="inproj",
    )(*args)
    return outs


def _attn_kernel(*refs, l, local, nb):
    if local:
        qt_ref, kp_ref, km_ref, kn_ref, vp_ref, vm_ref, vn_ref, kc_ref, vc_ref, sink_ref, yt_ref = refs
    else:
        qt_ref, kc_ref, vc_ref, sink_ref, yt_ref = refs
    nsub = qt_ref.shape[1] // BLK
    ty = yt_ref.shape[2]
    grp = N_HEADS // N_KV_HEADS
    gsub = ATTN_HEADS_PER_ITEM
    width = gsub * BLK
    i = pl.program_id(1)
    if local:
        kj = lax.broadcasted_iota(jnp.int32, (BLK, width), 0)
        qi = lax.broadcasted_iota(jnp.int32, (BLK, width), 1) & (BLK - 1)
        band_prev, band_next = kj >= qi, kj <= qi
        edge_prev = kj >= qi + jnp.where(i > 0, 0, BLK)
        edge_next = kj <= qi - jnp.where(i < nb - 1, 0, BLK)
    zeros = jnp.zeros((ROT_HALF, width), BF16)

    def key_chunks(j, lanes):
        ks = [kc_ref[:, lanes]]
        if local:
            blk = lambda t: slice(t * BLK, (t + 1) * BLK)
            ks.append(kp_ref[:, lanes] if j == 0 else km_ref[blk(j - 1), lanes])
            ks.append(km_ref[blk(j), lanes])
            ks.append(kn_ref[:, lanes] if j == nsub - 1 else km_ref[blk(j + 1), lanes])
        return ks

    def value_chunks(j, rows):
        vs = [vc_ref[t, rows, :] for t in range(vc_ref.shape[0])]
        if local:
            vs.append(vp_ref[rows, :] if j == 0 else vm_ref[j - 1, rows, :])
            vs.append(vm_ref[j, rows, :])
            vs.append(vn_ref[rows, :] if j == nsub - 1 else vm_ref[j + 1, rows, :])
        return vs

    def scores(j, kh, g0):
        pair, half = kh // 2, kh % 2
        cols = slice(j * BLK, (j + 1) * BLK)
        def piece(r, hi):
            h = kh * grp + g0 + r
            row0 = (h // 2) * LANES + hi * HEAD_DIM + (h % 2) * ROT_HALF
            return qt_ref[row0:row0 + ROT_HALF, cols]
        q_lo = jnp.concatenate([piece(r, 0) for r in range(gsub)], axis=1)
        q_hi = jnp.concatenate([piece(r, 1) for r in range(gsub)], axis=1)
        q_pad = jnp.concatenate([q_lo, zeros, q_hi, zeros] if half == 0 else [zeros, q_lo, zeros, q_hi], axis=0)
        ss = [jnp.dot(kk, q_pad, preferred_element_type=F32)
              for kk in key_chunks(j, slice(pair * LANES, (pair + 1) * LANES))]
        if local:
            ss[1] = jnp.where(edge_prev if j == 0 else band_prev, ss[1], NEG)
            ss[3] = jnp.where(edge_next if j == nsub - 1 else band_next, ss[3], NEG)
        m = jnp.max(ss[0], axis=0, keepdims=True)
        for s_x in ss[1:]:
            m = jnp.maximum(m, jnp.max(s_x, axis=0, keepdims=True))
        return ss, m

    def softmax(kh, g0, scored):
        ss, m = scored
        sink = jnp.concatenate(
            [jnp.full((1, BLK), sink_ref[l, kh * grp + g0 + r] * LOG2E, F32) for r in range(gsub)], axis=1)
        m = jnp.maximum(m, sink)
        p_t = jnp.concatenate([jnp.exp2((s_x - m).astype(BF16)) for s_x in ss], axis=0)
        return p_t, jnp.exp2(sink - m)

    def weighted_values(j, kh, g0, p_t, p_sink):
        v_t = jnp.concatenate(value_chunks(j, slice(kh * HEAD_DIM, (kh + 1) * HEAD_DIM)), axis=1)
        v_ext = jnp.concatenate([v_t, jnp.ones((HALO, v_t.shape[1]), BF16)], axis=0)
        o_ext = jnp.dot(v_ext, p_t, preferred_element_type=F32)
        o_t = o_ext[:HEAD_DIM, :] / (o_ext[HEAD_DIM:HEAD_DIM + 1, :] + p_sink)
        for r in range(gsub):
            h = kh * grp + g0 + r
            t0 = j * BLK
            yt_ref[t0 // ty, h * HEAD_DIM:(h + 1) * HEAD_DIM, t0 % ty:t0 % ty + BLK] = (
                o_t[:, r * BLK:(r + 1) * BLK].astype(BF16))

    items = [(j, kh, g0) for j in range(nsub) for kh in range(N_KV_HEADS) for g0 in range(0, grp, gsub)]
    n_items = len(items)
    scored = {t: scores(*items[t]) for t in range(ATTN_SCORE_LEAD)}
    probs = {t: softmax(*items[t][1:], scored.pop(t)) for t in range(ATTN_SOFTMAX_LEAD)}
    for n, (j, kh, g0) in enumerate(items):
        if n + ATTN_SCORE_LEAD < n_items:
            scored[n + ATTN_SCORE_LEAD] = scores(*items[n + ATTN_SCORE_LEAD])
        if n + ATTN_SOFTMAX_LEAD < n_items:
            t = n + ATTN_SOFTMAX_LEAD
            probs[t] = softmax(*items[t][1:], scored.pop(t))
        weighted_values(j, kh, g0, *probs.pop(n))


def _attention(l, q_t, kc, vc_t, sink, ty, k=None, v_t=None):
    b, nt, _, tq = q_t.shape
    s = nt * tq
    m = kc.shape[1]
    local = k is not None
    nsub, nblk = tq // BLK, s // BLK
    prev_b = lambda i: jnp.maximum(i * nsub - 1, 0)
    next_b = lambda i: jnp.minimum((i + 1) * nsub, nblk - 1)
    in_specs = [pl.BlockSpec((None, None, Q_W, tq), lambda bi, i: (bi, i, 0, 0))]
    args = [q_t]
    if local:
        in_specs += [
            pl.BlockSpec((None, BLK, KV_W), lambda bi, i: (bi, prev_b(i), 0)),
            pl.BlockSpec((None, tq, KV_W), lambda bi, i: (bi, i, 0)),
            pl.BlockSpec((None, BLK, KV_W), lambda bi, i: (bi, next_b(i), 0)),
            pl.BlockSpec((None, None, KV_W, BLK), lambda bi, i: (bi, prev_b(i), 0, 0)),
            pl.BlockSpec((None, nsub, KV_W, BLK), lambda bi, i: (bi, i, 0, 0)),
            pl.BlockSpec((None, None, KV_W, BLK), lambda bi, i: (bi, next_b(i), 0, 0)),
        ]
        args += [k, k, k, v_t, v_t, v_t]
    in_specs += [
        pl.BlockSpec((None, m, KV_W), lambda bi, i: (bi, 0, 0)),
        pl.BlockSpec((None, m // BLK, KV_W, BLK), lambda bi, i: (bi, 0, 0, 0)),
        pl.BlockSpec(memory_space=pltpu.SMEM),
    ]
    args += [kc, vc_t, sink]
    return pl.pallas_call(
        functools.partial(_attn_kernel, l=l, local=local, nb=nt),
        grid=(b, nt),
        in_specs=in_specs,
        out_specs=pl.BlockSpec((None, tq // ty, Q_W, ty), lambda bi, i: (bi, i, 0, 0)),
        out_shape=jax.ShapeDtypeStruct((b, s // ty, Q_W, ty), BF16),
        compiler_params=_cparams(2),
        name="attention",
    )(*args)


def _mix_kernel(x_ref, xp_ref, xn_ref, mod_ref, g_ref, yt_ref, w_ref, ws_ref, bs_ref, gv_ref, wc_ref,
                bg_ref, wb_ref, wo_ref, o_ref, ya_ref, *, nt):
    tm = x_ref.shape[0]
    i = pl.program_id(1)
    sh, sc = mod_ref[:, 0:D_MODEL], mod_ref[:, D_MODEL:2 * D_MODEL]

    def hmod(x):
        return ((_rms(x) * g_ref[...]) * (1.0 + sc) + sh).astype(BF16)

    x = x_ref[...]
    hm = hmod(x)
    he = jnp.concatenate([hmod(xp_ref[...]), hm, hmod(xn_ref[...])], axis=0)
    y_attn = lax.dot_general(yt_ref[...], wb_ref[0], (((0,), (0,)), ((), ())), preferred_element_type=F32)
    z_a = jnp.dot(hm, w_ref[:, MIX_A:MIX_BG], preferred_element_type=F32)
    z_p = jnp.dot(he, w_ref[:, MIX_P:MIX_G], preferred_element_type=F32)
    z_bg = jnp.dot(hm, w_ref[:, MIX_BG:MIX_P], preferred_element_type=F32)
    for n in range(tm // CHUNK):
        rows = slice(n * CHUNK, (n + 1) * CHUNK)
        z = jax.nn.gelu(z_a[rows, :])
        u, v = z[:, :A_W], z[:, A_W:]
        vn = (_rms(v) * gv_ref[...]).astype(BF16)
        for g in range(A_GROUPS):
            cols = slice(g * A_GW, (g + 1) * A_GW)
            mixed = jnp.dot(ws_ref[g], vn[:, cols], preferred_element_type=F32) + bs_ref[:, cols]
            ya_ref[rows, cols] = (u[:, cols] * mixed).astype(BF16)
    z_g = jnp.dot(hm, w_ref[:, MIX_G:], preferred_element_type=F32)
    prod = z_p[:, :B_W] * z_p[:, B_W:]
    r = lax.broadcasted_iota(jnp.int32, (tm, B_W), 0)
    first = r < jnp.where(i == 0, 1, 0)
    last = r >= tm - jnp.where(i == nt - 1, 1, 0)
    pm1 = jnp.where(first, 0.0, pltpu.roll(prod, 1, 0)[HALO:HALO + tm])
    pp1 = jnp.where(last, 0.0, pltpu.roll(prod, tm + 2 * HALO - 1, 0)[HALO:HALO + tm])
    y_b = z_bg * (wc_ref[0:1, :] * pm1 + wc_ref[1:2, :] * prod[HALO:HALO + tm] + wc_ref[2:3, :] * pp1)
    p_a = jnp.dot(ya_ref[...], wb_ref[1], preferred_element_type=F32)
    gates = jax.nn.sigmoid(z_g + bg_ref[...])
    p_b = jnp.dot(y_b.astype(BF16), wb_ref[2], preferred_element_type=F32)
    merged = (gates[:, :D_MODEL] * y_attn + gates[:, D_MODEL:2 * D_MODEL] * p_a + gates[:, 2 * D_MODEL:] * p_b)
    out = jnp.dot(merged.astype(BF16), wo_ref[...], preferred_element_type=F32)
    o_ref[...] = x + mod_ref[:, 2 * D_MODEL:3 * D_MODEL] * out


def _mix(l, x, mod, mod_row, y_t, g, w_abg, ws, bs, gv, wc, bg, wb, wo, tm):
    b, s, _ = x.shape
    nt = s // tm
    params = (w_abg, ws, bs, gv, wc, bg, wb, wo)
    in_specs = [pl.BlockSpec((None, tm, D_MODEL), lambda bi, i: (bi, i, 0))] + _halo_specs(tm, s) + [
        _mod_spec(l, mod_row),
        _layer_spec(g, l),
        pl.BlockSpec((None, None, Q_W, tm), lambda bi, i: (bi, i, 0, 0)),
    ] + [_layer_spec(a, l) for a in params]
    return pl.pallas_call(
        functools.partial(_mix_kernel, nt=nt),
        grid=(b, nt),
        in_specs=in_specs,
        out_specs=pl.BlockSpec((None, tm, D_MODEL), lambda bi, i: (bi, i, 0)),
        out_shape=jax.ShapeDtypeStruct((b, s, D_MODEL), F32),
        scratch_shapes=[pltpu.VMEM((tm, A_W), BF16)],
        compiler_params=_cparams(2),
        name="mix",
    )(x, x, x, mod, g, y_t, *params)


def _ffn_kernel(x_ref, xp_ref, xn_ref, mod_ref, g_ref, wu_ref, wc_ref, wd_ref, gf_ref, o_ref, act_ref,
                *, nt, final, fc):
    tm = x_ref.shape[0]
    i = pl.program_id(1)
    sh, sc = mod_ref[:, 3 * D_MODEL:4 * D_MODEL], mod_ref[:, 4 * D_MODEL:5 * D_MODEL]

    def hmod(x):
        return ((_rms(x) * g_ref[...]) * (1.0 + sc) + sh).astype(BF16)

    x = x_ref[...]
    hm = hmod(x)
    he = jnp.concatenate([hmod(xp_ref[...]), hm, hmod(xn_ref[...])], axis=0)
    r = lax.broadcasted_iota(jnp.int32, (tm, fc), 0)
    first = r < jnp.where(i == 0, 1, 0)
    last = r >= tm - jnp.where(i == nt - 1, 1, 0)
    for c in range(0, D_FF, fc):
        a = jnp.dot(he, wu_ref[:, c:c + fc], preferred_element_type=F32)
        gt = jnp.dot(hm, wu_ref[:, D_FF + c:D_FF + c + fc], preferred_element_type=F32)
        am1 = jnp.where(first, 0.0, pltpu.roll(a, 1, 0)[HALO:HALO + tm])
        ap1 = jnp.where(last, 0.0, pltpu.roll(a, tm + 2 * HALO - 1, 0)[HALO:HALO + tm])
        conv = (wc_ref[0:1, c:c + fc] * am1 + wc_ref[1:2, c:c + fc] * a[HALO:HALO + tm]
                + wc_ref[2:3, c:c + fc] * ap1)
        act_ref[:, c:c + fc] = (jax.nn.silu(conv) * gt).astype(BF16)
    y = jnp.dot(act_ref[...], wd_ref[...], preferred_element_type=F32)
    out = x + mod_ref[:, 5 * D_MODEL:6 * D_MODEL] * y
    if final:
        out = _rms(out) * gf_ref[...]
    o_ref[...] = out


def _ffn(l, x, mod, mod_row, g, wu, wc, wd, gf, tm, final):
    b, s, _ = x.shape
    nt = s // tm
    in_specs = [pl.BlockSpec((None, tm, D_MODEL), lambda bi, i: (bi, i, 0))] + _halo_specs(tm, s) + [
        _mod_spec(l, mod_row),
        _layer_spec(g, l), _layer_spec(wu, l), _layer_spec(wc, l), _layer_spec(wd, l),
        pl.BlockSpec(gf.shape, lambda bi, i: (0, 0)),
    ]
    return pl.pallas_call(
        functools.partial(_ffn_kernel, nt=nt, final=final, fc=256),
        grid=(b, nt),
        in_specs=in_specs,
        out_specs=pl.BlockSpec((None, tm, D_MODEL), lambda bi, i: (bi, i, 0)),
        out_shape=jax.ShapeDtypeStruct((b, s, D_MODEL), F32),
        scratch_shapes=[pltpu.VMEM((tm, D_FF), BF16)],
        compiler_params=_cparams(2),
        name="ffn",
    )(x, x, x, mod, g, wu, wc, wd, gf)


def _rope_tables(n):
    rows = n // GRID_W
    row = jnp.broadcast_to(jnp.arange(rows, dtype=F32)[:, None], (rows, GRID_W)).reshape(n)
    col = jnp.broadcast_to(jnp.arange(GRID_W, dtype=F32)[None, :], (rows, GRID_W)).reshape(n)
    half = HEAD_DIM // 2
    inv = ROPE_THETA ** (-jnp.arange(0, half, 2, dtype=F32) / half)
    ang = jnp.concatenate([row[:, None] * inv, col[:, None] * inv], axis=-1)
    cos, sin = jnp.cos(ang), jnp.sin(ang)
    cos_t = jnp.concatenate([cos, cos, cos, cos], axis=-1)
    sin_t = jnp.concatenate([-sin, -sin, sin, sin], axis=-1)
    return cos_t, sin_t


def _pair_interleave(w):
    lead = w.shape[:-1]
    w = w.reshape(lead + (-1, 2, 2, ROT_HALF))
    return jnp.swapaxes(w, -3, -2).reshape(lead + (-1,))


def kernel(x, c, ctx, c_ctx, w_mod, b_mod, g_mix, w_in, b_gate, sink, w_spatial, b_spatial, g_v, w_sconv,
           w_branch, w_out, g_ffn, w_up, w_fconv, w_down, g_final):
    b, n, _ = x.shape
    m = ctx.shape[1]
    assert b + 1 <= MOD_ROWS and n % 512 == 0 and m % CHUNK == 0

    c_all = jnp.zeros((MOD_ROWS, D_MODEL), F32).at[:b].set(c).at[CTX_ROW].set(c_ctx)
    mod = _modulation(c_all, w_mod, b_mod).reshape(DEPTH, MOD_ROWS, 1, 6 * D_MODEL)
    rope_tabs = _rope_tables(n)

    row = lambda a: a.reshape(DEPTH, 1, -1)
    g_mix_r = row(g_mix)
    w_in_b = w_in.astype(BF16)
    w_qkv_b = jnp.concatenate([_pair_interleave(w_in_b[..., OFF_Q:OFF_K]), _pair_interleave(w_in_b[..., OFF_K:OFF_V]),
                               w_in_b[..., OFF_V:OFF_A]], axis=-1)
    bs_full = jnp.repeat(jnp.transpose(b_spatial, (0, 2, 1)), A_GW, axis=2)
    mix_w = (g_mix_r, w_in_b, w_spatial.astype(BF16), bs_full, row(g_v), w_sconv,
             row(b_gate), w_branch.astype(BF16), w_out.astype(BF16))
    ffn_w = (row(g_ffn), w_up.astype(BF16), w_fconv, w_down.astype(BF16), g_final.reshape(1, D_MODEL))

    lat_row = lambda bi: bi
    ctx_row = lambda bi: CTX_ROW
    tq, tmix, tffn = 1024, 512, 1024
    qkv_sections = lambda t: ((OFF_Q, Q_W, Q_W, QK_SCALE, t), (OFF_K, KV_W, KV_W, 1.0, 0),
                              (OFF_V, KV_W, 0, 1.0, BLK))

    xc = ctx
    for l in range(DEPTH):
        last = l == DEPTH - 1
        q_t, k, v_t = _inproj(l, x, mod, lat_row, g_mix_r, w_qkv_b, qkv_sections(tq), tq, rope_tabs)
        ctx_out = _inproj(l, xc, mod, ctx_row, g_mix_r, w_qkv_b, qkv_sections(m)[1 if last else 0:], m)
        kc, vc_t = ctx_out[-2:]
        y_t = _attention(l, q_t, kc, vc_t, sink, tmix, k=k, v_t=v_t)
        x = _mix(l, x, mod, lat_row, y_t, *mix_w, tm=tmix)
        x = _ffn(l, x, mod, lat_row, *ffn_w, tm=tffn, final=last)
        if not last:
            yc_t = _attention(l, ctx_out[0], kc, vc_t, sink, m)
            xc = _mix(l, xc, mod, ctx_row, yc_t, *mix_w, tm=m)
            xc = _ffn(l, xc, mod, ctx_row, *ffn_w, tm=m, final=False)
    return x
```

```python
import functools

import jax
import jax.numpy as jnp
from jax import lax
from jax.experimental import pallas as pl
from jax.experimental.pallas import tpu as pltpu

F32 = jnp.float32
BF16 = jnp.bfloat16

D_MODEL = 1024
DEPTH = 2
GRID_W = 64
N_HEADS = 16
N_KV_HEADS = 4
HEAD_DIM = 64
ROT_HALF = HEAD_DIM // 2
Q_W = N_HEADS * HEAD_DIM
KV_W = N_KV_HEADS * HEAD_DIM
BLK = 128
ROPE_THETA = 10000.0
CHUNK = 128
A_W = D_MODEL
A_GROUPS = 8
A_GW = A_W // A_GROUPS
B_W = D_MODEL
N_BRANCH = 3
D_FF = 2816
EPS = 1e-6
NEG = -1e30
OFF_Q = 0
OFF_K = OFF_Q + Q_W
OFF_V = OFF_K + KV_W
OFF_A = OFF_V + KV_W
OFF_B = OFF_A + 2 * A_W
OFF_G = OFF_B + 3 * B_W
IN_W = OFF_G + N_BRANCH * D_MODEL
MIX_A, MIX_BG, MIX_P, MIX_G = OFF_A, OFF_B, OFF_B + B_W, OFF_G

LOG2E = 1.4426950408889634
QK_SCALE = HEAD_DIM ** -0.5 * LOG2E

LANES = 128
HALO = 16
VMEM_LIMIT = 56 * 1024 * 1024
MOD_ROWS = 8
CTX_ROW = 4
ATTN_HEADS_PER_ITEM = 2
ATTN_INTERLEAVE = 4


def _cparams(n_axes):
    return pltpu.CompilerParams(
        dimension_semantics=("arbitrary",) * n_axes, vmem_limit_bytes=VMEM_LIMIT)


def _rms(x):
    return x * lax.rsqrt(jnp.mean(x * x, axis=-1, keepdims=True) + EPS)


def _layer_spec(a, l):
    return pl.BlockSpec((None,) + a.shape[1:], lambda bi, i: (l,) + (0,) * (a.ndim - 1),
                        pipeline_mode=pl.Buffered(1))


def _mod_spec(l, mod_row):
    return pl.BlockSpec((None, None, 1, 6 * D_MODEL), lambda bi, i: (l, mod_row(bi), 0, 0))


def _halo_specs(tm, s):
    per_tile, n_halo = tm // HALO, s // HALO
    return [
        pl.BlockSpec((None, HALO, D_MODEL), lambda bi, i: (bi, jnp.maximum(i * per_tile - 1, 0), 0)),
        pl.BlockSpec((None, HALO, D_MODEL), lambda bi, i: (bi, jnp.minimum((i + 1) * per_tile, n_halo - 1), 0)),
    ]


def _mod_kernel(c_ref, w_ref, b_ref, o_ref):
    a = jax.nn.silu(c_ref[...]).astype(BF16)
    o_ref[...] = jnp.dot(a, w_ref[...].astype(BF16), preferred_element_type=F32) + b_ref[...]


def _modulation(c_all, w_mod, b_mod):
    tn = 1536
    n_out = 6 * D_MODEL
    return pl.pallas_call(
        _mod_kernel,
        grid=(DEPTH, n_out // tn),
        in_specs=[
            pl.BlockSpec((MOD_ROWS, D_MODEL), lambda l, j: (0, 0)),
            pl.BlockSpec((None, D_MODEL, tn), lambda l, j: (l, 0, j)),
            pl.BlockSpec((None, 1, tn), lambda l, j: (l, 0, j)),
        ],
        out_specs=pl.BlockSpec((None, MOD_ROWS, tn), lambda l, j: (l, 0, j)),
        out_shape=jax.ShapeDtypeStruct((DEPTH, MOD_ROWS, n_out), F32),
        compiler_params=_cparams(2),
        name="modulation",
    )(c_all, w_mod, b_mod.reshape(DEPTH, 1, n_out))


def _inproj_kernel(*refs, sections, rope, tn):
    if rope:
        x_ref, mod_ref, g_ref, cos_ref, sin_ref, w_ref = refs[:6]
        out_refs = refs[6:]
    else:
        x_ref, mod_ref, g_ref, w_ref = refs[:4]
        out_refs = refs[4:]
    h = _rms(x_ref[...]) * g_ref[...]
    h = h * (1.0 + mod_ref[:, D_MODEL:2 * D_MODEL]) + mod_ref[:, 0:D_MODEL]
    hb = h.astype(BF16)
    for (start, width, rope_w, scale, tile), o_ref in zip(sections, out_refs):
        for c0 in range(0, width, tn):
            w = min(tn, width - c0)
            acc = jnp.dot(hb, w_ref[:, start + c0:start + c0 + w], preferred_element_type=F32)
            for cc in range(0, w, LANES):
                t = acc[:, cc:cc + LANES]
                if rope and c0 + cc < rope_w:
                    t = t * cos_ref[...] + pltpu.roll(t, HEAD_DIM, 1) * sin_ref[...]
                if scale != 1.0:
                    t = t * scale
                if tile:
                    for jj in range(t.shape[0] // tile):
                        o_ref[jj, c0 + cc:c0 + cc + LANES, :] = jnp.transpose(
                            t[jj * tile:(jj + 1) * tile, :]).astype(BF16)
                else:
                    o_ref[:, c0 + cc:c0 + cc + LANES] = t.astype(BF16)


def _inproj(l, x, mod, mod_row, g, w, sections, tm, rope_tabs=None):
    b, s, _ = x.shape
    nt = s // tm
    rope = rope_tabs is not None
    in_specs = [
        pl.BlockSpec((None, tm, D_MODEL), lambda bi, i: (bi, i, 0)),
        _mod_spec(l, mod_row),
        _layer_spec(g, l),
    ]
    args = [x, mod, g]
    if rope:
        in_specs += [pl.BlockSpec((tm, LANES), lambda bi, i: (i, 0))] * 2
        args += list(rope_tabs)
    in_specs.append(_layer_spec(w, l))
    args.append(w)
    outs = pl.pallas_call(
        functools.partial(_inproj_kernel, sections=tuple(sections), rope=rope, tn=512),
        grid=(b, nt),
        in_specs=in_specs,
        out_specs=[pl.BlockSpec((None, tm // sec[4], sec[1], sec[4]), lambda bi, i: (bi, i, 0, 0)) if sec[4]
                   else pl.BlockSpec((None, tm, sec[1]), lambda bi, i: (bi, i, 0)) for sec in sections],
        out_shape=[jax.ShapeDtypeStruct((b, s // sec[4], sec[1], sec[4]) if sec[4] else (b, s, sec[1]), BF16)
                   for sec in sections],
        compiler_params=_cparams(2),
        name="inproj",
    )(*args)
    return outs


def _attn_kernel(*refs, l, local, nb):
    if local:
        qt_ref, kp_ref, km_ref, kn_ref, vp_ref, vm_ref, vn_ref, kc_ref, vc_ref, sink_ref, yt_ref = refs
    else:
        qt_ref, kc_ref, vc_ref, sink_ref, yt_ref = refs
    nsub = qt_ref.shape[1] // BLK
    ty = yt_ref.shape[2]
    grp = N_HEADS // N_KV_HEADS
    gsub = ATTN_HEADS_PER_ITEM
    width = gsub * BLK
    i = pl.program_id(1)
    if local:
        kj = lax.broadcasted_iota(jnp.int32, (BLK, width), 0)
        qi = lax.broadcasted_iota(jnp.int32, (BLK, width), 1) & (BLK - 1)
        band_prev, band_next = kj >= qi, kj <= qi
        edge_prev = kj >= qi + jnp.where(i > 0, 0, BLK)
        edge_next = kj <= qi - jnp.where(i < nb - 1, 0, BLK)
    zeros = jnp.zeros((ROT_HALF, width), BF16)

    def key_value_chunks(j, lanes, rows):
        ones = jnp.ones((HALO, BLK), BF16)
        ext = lambda v: jnp.concatenate([v, ones], axis=0)
        out = [(kc_ref[t * BLK:(t + 1) * BLK, lanes], ext(vc_ref[t, rows, :]), None)
               for t in range(vc_ref.shape[0])]
        if local:
            blk = lambda t: slice(t * BLK, (t + 1) * BLK)
            out.append((kp_ref[:, lanes], ext(vp_ref[rows, :]), edge_prev) if j == 0
                       else (km_ref[blk(j - 1), lanes], ext(vm_ref[j - 1, rows, :]), band_prev))
            out.append((km_ref[blk(j), lanes], ext(vm_ref[j, rows, :]), None))
            out.append((kn_ref[:, lanes], ext(vn_ref[rows, :]), edge_next) if j == nsub - 1
                       else (km_ref[blk(j + 1), lanes], ext(vm_ref[j + 1, rows, :]), band_next))
        return out

    def start(j, kh, g0):
        pair, half = kh // 2, kh % 2
        cols = slice(j * BLK, (j + 1) * BLK)
        def piece(r, hi):
            h = kh * grp + g0 + r
            row0 = (h // 2) * LANES + hi * HEAD_DIM + (h % 2) * ROT_HALF
            return qt_ref[row0:row0 + ROT_HALF, cols]
        q_lo = jnp.concatenate([piece(r, 0) for r in range(gsub)], axis=1)
        q_hi = jnp.concatenate([piece(r, 1) for r in range(gsub)], axis=1)
        q_pad = jnp.concatenate([q_lo, zeros, q_hi, zeros] if half == 0 else [zeros, q_lo, zeros, q_hi], axis=0)
        sink = jnp.concatenate(
            [jnp.full((1, BLK), sink_ref[l, kh * grp + g0 + r] * LOG2E, F32) for r in range(gsub)], axis=1)
        chunks = key_value_chunks(j, slice(pair * LANES, (pair + 1) * LANES),
                                  slice(kh * HEAD_DIM, (kh + 1) * HEAD_DIM))
        return dict(q=q_pad, sink=sink, m=sink, acc=jnp.zeros((HEAD_DIM + HALO, width), F32), chunks=chunks)

    def step(st, c):
        k_c, v_c, mask = st["chunks"][c]
        s_c = jnp.dot(k_c, st["q"], preferred_element_type=F32)
        if mask is not None:
            s_c = jnp.where(mask, s_c, NEG)
        m_new = jnp.maximum(st["m"], jnp.max(s_c, axis=0, keepdims=True))
        p_c = jnp.exp2((s_c - m_new).astype(BF16))
        st["acc"] = st["acc"] * jnp.exp2(st["m"] - m_new) + jnp.dot(v_c, p_c, preferred_element_type=F32)
        st["m"] = m_new

    def finish(j, kh, g0, st):
        den = st["acc"][HEAD_DIM:HEAD_DIM + 1, :] + jnp.exp2(st["sink"] - st["m"])
        o_t = st["acc"][:HEAD_DIM, :] / den
        for r in range(gsub):
            h = kh * grp + g0 + r
            t0 = j * BLK
            yt_ref[t0 // ty, h * HEAD_DIM:(h + 1) * HEAD_DIM, t0 % ty:t0 % ty + BLK] = (
                o_t[:, r * BLK:(r + 1) * BLK].astype(BF16))

    items = [(j, kh, g0) for j in range(nsub) for kh in range(N_KV_HEADS) for g0 in range(0, grp, gsub)]
    for n0 in range(0, len(items), ATTN_INTERLEAVE):
        group = items[n0:n0 + ATTN_INTERLEAVE]
        states = [start(*it) for it in group]
        for c in range(len(states[0]["chunks"])):
            for st in states:
                step(st, c)
        for it, st in zip(group, states):
            finish(*it, st)


def _attention(l, q_t, kc, vc_t, sink, ty, k=None, v_t=None):
    b, nt, _, tq = q_t.shape
    s = nt * tq
    m = kc.shape[1]
    local = k is not None
    nsub, nblk = tq // BLK, s // BLK
    prev_b = lambda i: jnp.maximum(i * nsub - 1, 0)
    next_b = lambda i: jnp.minimum((i + 1) * nsub, nblk - 1)
    in_specs = [pl.BlockSpec((None, None, Q_W, tq), lambda bi, i: (bi, i, 0, 0))]
    args = [q_t]
    if local:
        in_specs += [
            pl.BlockSpec((None, BLK, KV_W), lambda bi, i: (bi, prev_b(i), 0)),
            pl.BlockSpec((None, tq, KV_W), lambda bi, i: (bi, i, 0)),
            pl.BlockSpec((None, BLK, KV_W), lambda bi, i: (bi, next_b(i), 0)),
            pl.BlockSpec((None, None, KV_W, BLK), lambda bi, i: (bi, prev_b(i), 0, 0)),
            pl.BlockSpec((None, nsub, KV_W, BLK), lambda bi, i: (bi, i, 0, 0)),
            pl.BlockSpec((None, None, KV_W, BLK), lambda bi, i: (bi, next_b(i), 0, 0)),
        ]
        args += [k, k, k, v_t, v_t, v_t]
    in_specs += [
        pl.BlockSpec((None, m, KV_W), lambda bi, i: (bi, 0, 0)),
        pl.BlockSpec((None, m // BLK, KV_W, BLK), lambda bi, i: (bi, 0, 0, 0)),
        pl.BlockSpec(memory_space=pltpu.SMEM),
    ]
    args += [kc, vc_t, sink]
    return pl.pallas_call(
        functools.partial(_attn_kernel, l=l, local=local, nb=nt),
        grid=(b, nt),
        in_specs=in_specs,
        out_specs=pl.BlockSpec((None, tq // ty, Q_W, ty), lambda bi, i: (bi, i, 0, 0)),
        out_shape=jax.ShapeDtypeStruct((b, s // ty, Q_W, ty), BF16),
        compiler_params=_cparams(2),
        name="attention",
    )(*args)


def _mix_kernel(x_ref, xp_ref, xn_ref, mod_ref, g_ref, yt_ref, w_ref, ws_ref, bs_ref, gv_ref, wc_ref,
                bg_ref, wb_ref, wo_ref, o_ref, ya_ref, *, nt):
    tm = x_ref.shape[0]
    i = pl.program_id(1)
    sh, sc = mod_ref[:, 0:D_MODEL], mod_ref[:, D_MODEL:2 * D_MODEL]

    def hmod(x):
        return ((_rms(x) * g_ref[...]) * (1.0 + sc) + sh).astype(BF16)

    x = x_ref[...]
    hm = hmod(x)
    he = jnp.concatenate([hmod(xp_ref[...]), hm, hmod(xn_ref[...])], axis=0)
    y_attn = lax.dot_general(yt_ref[...], wb_ref[0], (((0,), (0,)), ((), ())), preferred_element_type=F32)
    z_a = jnp.dot(hm, w_ref[:, MIX_A:MIX_BG], preferred_element_type=F32)
    z_p = jnp.dot(he, w_ref[:, MIX_P:MIX_G], preferred_element_type=F32)
    z_bg = jnp.dot(hm, w_ref[:, MIX_BG:MIX_P], preferred_element_type=F32)
    for n in range(tm // CHUNK):
        rows = slice(n * CHUNK, (n + 1) * CHUNK)
        z = jax.nn.gelu(z_a[rows, :])
        u, v = z[:, :A_W], z[:, A_W:]
        vn = (_rms(v) * gv_ref[...]).astype(BF16)
        for g in range(A_GROUPS):
            cols = slice(g * A_GW, (g + 1) * A_GW)
            mixed = jnp.dot(ws_ref[g], vn[:, cols], preferred_element_type=F32) + bs_ref[:, cols]
            ya_ref[rows, cols] = (u[:, cols] * mixed).astype(BF16)
    z_g = jnp.dot(hm, w_ref[:, MIX_G:], preferred_element_type=F32)
    prod = z_p[:, :B_W] * z_p[:, B_W:]
    r = lax.broadcasted_iota(jnp.int32, (tm, B_W), 0)
    first = r < jnp.where(i == 0, 1, 0)
    last = r >= tm - jnp.where(i == nt - 1, 1, 0)
    pm1 = jnp.where(first, 0.0, pltpu.roll(prod, 1, 0)[HALO:HALO + tm])
    pp1 = jnp.where(last, 0.0, pltpu.roll(prod, tm + 2 * HALO - 1, 0)[HALO:HALO + tm])
    y_b = z_bg * (wc_ref[0:1, :] * pm1 + wc_ref[1:2, :] * prod[HALO:HALO + tm] + wc_ref[2:3, :] * pp1)
    p_a = jnp.dot(ya_ref[...], wb_ref[1], preferred_element_type=F32)
    gates = jax.nn.sigmoid(z_g + bg_ref[...])
    p_b = jnp.dot(y_b.astype(BF16), wb_ref[2], preferred_element_type=F32)
    merged = (gates[:, :D_MODEL] * y_attn + gates[:, D_MODEL:2 * D_MODEL] * p_a + gates[:, 2 * D_MODEL:] * p_b)
    out = jnp.dot(merged.astype(BF16), wo_ref[...], preferred_element_type=F32)
    o_ref[...] = x + mod_ref[:, 2 * D_MODEL:3 * D_MODEL] * out


def _mix(l, x, mod, mod_row, y_t, g, w_abg, ws, bs, gv, wc, bg, wb, wo, tm):
    b, s, _ = x.shape
    nt = s // tm
    params = (w_abg, ws, bs, gv, wc, bg, wb, wo)
    in_specs = [pl.BlockSpec((None, tm, D_MODEL), lambda bi, i: (bi, i, 0))] + _halo_specs(tm, s) + [
        _mod_spec(l, mod_row),
        _layer_spec(g, l),
        pl.BlockSpec((None, None, Q_W, tm), lambda bi, i: (bi, i, 0, 0)),
    ] + [_layer_spec(a, l) for a in params]
    return pl.pallas_call(
        functools.partial(_mix_kernel, nt=nt),
        grid=(b, nt),
        in_specs=in_specs,
        out_specs=pl.BlockSpec((None, tm, D_MODEL), lambda bi, i: (bi, i, 0)),
        out_shape=jax.ShapeDtypeStruct((b, s, D_MODEL), F32),
        scratch_shapes=[pltpu.VMEM((tm, A_W), BF16)],
        compiler_params=_cparams(2),
        name="mix",
    )(x, x, x, mod, g, y_t, *params)


def _ffn_kernel(x_ref, xp_ref, xn_ref, mod_ref, g_ref, wu_ref, wc_ref, wd_ref, gf_ref, o_ref, act_ref,
                *, nt, final, fc):
    tm = x_ref.shape[0]
    i = pl.program_id(1)
    sh, sc = mod_ref[:, 3 * D_MODEL:4 * D_MODEL], mod_ref[:, 4 * D_MODEL:5 * D_MODEL]

    def hmod(x):
        return ((_rms(x) * g_ref[...]) * (1.0 + sc) + sh).astype(BF16)

    x = x_ref[...]
    hm = hmod(x)
    he = jnp.concatenate([hmod(xp_ref[...]), hm, hmod(xn_ref[...])], axis=0)
    r = lax.broadcasted_iota(jnp.int32, (tm, fc), 0)
    first = r < jnp.where(i == 0, 1, 0)
    last = r >= tm - jnp.where(i == nt - 1, 1, 0)
    for c in range(0, D_FF, fc):
        a = jnp.dot(he, wu_ref[:, c:c + fc], preferred_element_type=F32)
        gt = jnp.dot(hm, wu_ref[:, D_FF + c:D_FF + c + fc], preferred_element_type=F32)
        am1 = jnp.where(first, 0.0, pltpu.roll(a, 1, 0)[HALO:HALO + tm])
        ap1 = jnp.where(last, 0.0, pltpu.roll(a, tm + 2 * HALO - 1, 0)[HALO:HALO + tm])
        conv = (wc_ref[0:1, c:c + fc] * am1 + wc_ref[1:2, c:c + fc] * a[HALO:HALO + tm]
                + wc_ref[2:3, c:c + fc] * ap1)
        act_ref[:, c:c + fc] = (jax.nn.silu(conv) * gt).astype(BF16)
    y = jnp.dot(act_ref[...], wd_ref[...], preferred_element_type=F32)
    out = x + mod_ref[:, 5 * D_MODEL:6 * D_MODEL] * y
    if final:
        out = _rms(out) * gf_ref[...]
    o_ref[...] = out


def _ffn(l, x, mod, mod_row, g, wu, wc, wd, gf, tm, final):
    b, s, _ = x.shape
    nt = s // tm
    in_specs = [pl.BlockSpec((None, tm, D_MODEL), lambda bi, i: (bi, i, 0))] + _halo_specs(tm, s) + [
        _mod_spec(l, mod_row),
        _layer_spec(g, l), _layer_spec(wu, l), _layer_spec(wc, l), _layer_spec(wd, l),
        pl.BlockSpec(gf.shape, lambda bi, i: (0, 0)),
    ]
    return pl.pallas_call(
        functools.partial(_ffn_kernel, nt=nt, final=final, fc=256),
        grid=(b, nt),
        in_specs=in_specs,
        out_specs=pl.BlockSpec((None, tm, D_MODEL), lambda bi, i: (bi, i, 0)),
        out_shape=jax.ShapeDtypeStruct((b, s, D_MODEL), F32),
        scratch_shapes=[pltpu.VMEM((tm, D_FF), BF16)],
        compiler_params=_cparams(2),
        name="ffn",
    )(x, x, x, mod, g, wu, wc, wd, gf)


def _rope_tables(n):
    rows = n // GRID_W
    row = jnp.broadcast_to(jnp.arange(rows, dtype=F32)[:, None], (rows, GRID_W)).reshape(n)
    col = jnp.broadcast_to(jnp.arange(GRID_W, dtype=F32)[None, :], (rows, GRID_W)).reshape(n)
    half = HEAD_DIM // 2
    inv = ROPE_THETA ** (-jnp.arange(0, half, 2, dtype=F32) / half)
    ang = jnp.concatenate([row[:, None] * inv, col[:, None] * inv], axis=-1)
    cos, sin = jnp.cos(ang), jnp.sin(ang)
    cos_t = jnp.concatenate([cos, cos, cos, cos], axis=-1)
    sin_t = jnp.concatenate([-sin, -sin, sin, sin], axis=-1)
    return cos_t, sin_t


def _pair_interleave(w):
    lead = w.shape[:-1]
    w = w.reshape(lead + (-1, 2, 2, ROT_HALF))
    return jnp.swapaxes(w, -3, -2).reshape(lead + (-1,))


def kernel(x, c, ctx, c_ctx, w_mod, b_mod, g_mix, w_in, b_gate, sink, w_spatial, b_spatial, g_v, w_sconv,
           w_branch, w_out, g_ffn, w_up, w_fconv, w_down, g_final):
    b, n, _ = x.shape
    m = ctx.shape[1]
    assert b + 1 <= MOD_ROWS and n % 512 == 0 and m % CHUNK == 0

    c_all = jnp.zeros((MOD_ROWS, D_MODEL), F32).at[:b].set(c).at[CTX_ROW].set(c_ctx)
    mod = _modulation(c_all, w_mod, b_mod).reshape(DEPTH, MOD_ROWS, 1, 6 * D_MODEL)
    rope_tabs = _rope_tables(n)

    row = lambda a: a.reshape(DEPTH, 1, -1)
    g_mix_r = row(g_mix)
    w_in_b = w_in.astype(BF16)
    w_qkv_b = jnp.concatenate([_pair_interleave(w_in_b[..., OFF_Q:OFF_K]), _pair_interleave(w_in_b[..., OFF_K:OFF_V]),
                               w_in_b[..., OFF_V:OFF_A]], axis=-1)
    bs_full = jnp.repeat(jnp.transpose(b_spatial, (0, 2, 1)), A_GW, axis=2)
    mix_w = (g_mix_r, w_in_b, w_spatial.astype(BF16), bs_full, row(g_v), w_sconv,
             row(b_gate), w_branch.astype(BF16), w_out.astype(BF16))
    ffn_w = (row(g_ffn), w_up.astype(BF16), w_fconv, w_down.astype(BF16), g_final.reshape(1, D_MODEL))

    lat_row = lambda bi: bi
    ctx_row = lambda bi: CTX_ROW
    tq, tmix, tffn = 1024, 512, 1024
    qkv_sections = lambda t: ((OFF_Q, Q_W, Q_W, QK_SCALE, t), (OFF_K, KV_W, KV_W, 1.0, 0),
                              (OFF_V, KV_W, 0, 1.0, BLK))

    xc = ctx
    for l in range(DEPTH):
        last = l == DEPTH - 1
        q_t, k, v_t = _inproj(l, x, mod, lat_row, g_mix_r, w_qkv_b, qkv_sections(tq), tq, rope_tabs)
        ctx_out = _inproj(l, xc, mod, ctx_row, g_mix_r, w_qkv_b, qkv_sections(m)[1 if last else 0:], m)
        kc, vc_t = ctx_out[-2:]
        y_t = _attention(l, q_t, kc, vc_t, sink, tmix, k=k, v_t=v_t)
        x = _mix(l, x, mod, lat_row, y_t, *mix_w, tm=tmix)
        x = _ffn(l, x, mod, lat_row, *ffn_w, tm=tffn, final=last)
        if not last:
            yc_t = _attention(l, ctx_out[0], kc, vc_t, sink, m)
            xc = _mix(l, xc, mod, ctx_row, yc_t, *mix_w, tm=m)
            xc = _ffn(l, xc, mod, ctx_row, *ffn_w, tm=m, final=False)
    return x
```

```python
import functools

import jax
import jax.numpy as jnp
from jax import lax
from jax.experimental import pallas as pl
from jax.experimental.pallas import tpu as pltpu

F32 = jnp.float32
BF16 = jnp.bfloat16

D_MODEL = 1024
DEPTH = 2
GRID_W = 64
N_HEADS = 16
N_KV_HEADS = 4
HEAD_DIM = 64
ROT_HALF = HEAD_DIM // 2
Q_W = N_HEADS * HEAD_DIM
KV_W = N_KV_HEADS * HEAD_DIM
BLK = 128
ROPE_THETA = 10000.0
CHUNK = 128
A_W = D_MODEL
A_GROUPS = 8
A_GW = A_W // A_GROUPS
B_W = D_MODEL
N_BRANCH = 3
D_FF = 2816
EPS = 1e-6
NEG = -1e30
OFF_Q = 0
OFF_K = OFF_Q + Q_W
OFF_V = OFF_K + KV_W
OFF_A = OFF_V + KV_W
OFF_B = OFF_A + 2 * A_W
OFF_G = OFF_B + 3 * B_W
IN_W = OFF_G + N_BRANCH * D_MODEL
MIX_A, MIX_BG, MIX_P, MIX_G = OFF_A, OFF_B, OFF_B + B_W, OFF_G

LOG2E = 1.4426950408889634
QK_SCALE = HEAD_DIM ** -0.5 * LOG2E

LANES = 128
HALO = 16
VMEM_LIMIT = 56 * 1024 * 1024
MOD_ROWS = 8
CTX_ROW = 4
ATTN_HEADS_PER_ITEM = 2
ATTN_INTERLEAVE = 8


def _cparams(n_axes):
    return pltpu.CompilerParams(
        dimension_semantics=("arbitrary",) * n_axes, vmem_limit_bytes=VMEM_LIMIT)


def _rms(x):
    return x * lax.rsqrt(jnp.mean(x * x, axis=-1, keepdims=True) + EPS)


def _layer_spec(a, l):
    return pl.BlockSpec((None,) + a.shape[1:], lambda bi, i: (l,) + (0,) * (a.ndim - 1),
                        pipeline_mode=pl.Buffered(1))


def _mod_spec(l, mod_row):
    return pl.BlockSpec((None, None, 1, 6 * D_MODEL), lambda bi, i: (l, mod_row(bi), 0, 0))


def _halo_specs(tm, s):
    per_tile, n_halo = tm // HALO, s // HALO
    return [
        pl.BlockSpec((None, HALO, D_MODEL), lambda bi, i: (bi, jnp.maximum(i * per_tile - 1, 0), 0)),
        pl.BlockSpec((None, HALO, D_MODEL), lambda bi, i: (bi, jnp.minimum((i + 1) * per_tile, n_halo - 1), 0)),
    ]


def _mod_kernel(c_ref, w_ref, b_ref, o_ref):
    a = jax.nn.silu(c_ref[...]).astype(BF16)
    o_ref[...] = jnp.dot(a, w_ref[...].astype(BF16), preferred_element_type=F32) + b_ref[...]


def _modulation(c_all, w_mod, b_mod):
    tn = 1536
    n_out = 6 * D_MODEL
    return pl.pallas_call(
        _mod_kernel,
        grid=(DEPTH, n_out // tn),
        in_specs=[
            pl.BlockSpec((MOD_ROWS, D_MODEL), lambda l, j: (0, 0)),
            pl.BlockSpec((None, D_MODEL, tn), lambda l, j: (l, 0, j)),
            pl.BlockSpec((None, 1, tn), lambda l, j: (l, 0, j)),
        ],
        out_specs=pl.BlockSpec((None, MOD_ROWS, tn), lambda l, j: (l, 0, j)),
        out_shape=jax.ShapeDtypeStruct((DEPTH, MOD_ROWS, n_out), F32),
        compiler_params=_cparams(2),
        name="modulation",
    )(c_all, w_mod, b_mod.reshape(DEPTH, 1, n_out))


def _inproj_kernel(*refs, sections, rope, tn):
    if rope:
        x_ref, mod_ref, g_ref, cos_ref, sin_ref, w_ref = refs[:6]
        out_refs = refs[6:]
    else:
        x_ref, mod_ref, g_ref, w_ref = refs[:4]
        out_refs = refs[4:]
    h = _rms(x_ref[...]) * g_ref[...]
    h = h * (1.0 + mod_ref[:, D_MODEL:2 * D_MODEL]) + mod_ref[:, 0:D_MODEL]
    hb = h.astype(BF16)
    for (start, width, rope_w, scale, tile), o_ref in zip(sections, out_refs):
        for c0 in range(0, width, tn):
            w = min(tn, width - c0)
            acc = jnp.dot(hb, w_ref[:, start + c0:start + c0 + w], preferred_element_type=F32)
            for cc in range(0, w, LANES):
                t = acc[:, cc:cc + LANES]
                if rope and c0 + cc < rope_w:
                    t = t * cos_ref[...] + pltpu.roll(t, HEAD_DIM, 1) * sin_ref[...]
                if scale != 1.0:
                    t = t * scale
                if tile:
                    for jj in range(t.shape[0] // tile):
                        o_ref[jj, c0 + cc:c0 + cc + LANES, :] = jnp.transpose(
                            t[jj * tile:(jj + 1) * tile, :]).astype(BF16)
                else:
                    o_ref[:, c0 + cc:c0 + cc + LANES] = t.astype(BF16)


def _inproj(l, x, mod, mod_row, g, w, sections, tm, rope_tabs=None):
    b, s, _ = x.shape
    nt = s // tm
    rope = rope_tabs is not None
    in_specs = [
        pl.BlockSpec((None, tm, D_MODEL), lambda bi, i: (bi, i, 0)),
        _mod_spec(l, mod_row),
        _layer_spec(g, l),
    ]
    args = [x, mod, g]
    if rope:
        in_specs += [pl.BlockSpec((tm, LANES), lambda bi, i: (i, 0))] * 2
        args += list(rope_tabs)
    in_specs.append(_layer_spec(w, l))
    args.append(w)
    outs = pl.pallas_call(
        functools.partial(_inproj_kernel, sections=tuple(sections), rope=rope, tn=512),
        grid=(b, nt),
        in_specs=in_specs,
        out_specs=[pl.BlockSpec((None, tm // sec[4], sec[1], sec[4]), lambda bi, i: (bi, i, 0, 0)) if sec[4]
                   else pl.BlockSpec((None, tm, sec[1]), lambda bi, i: (bi, i, 0)) for sec in sections],
        out_shape=[jax.ShapeDtypeStruct((b, s // sec[4], sec[1], sec[4]) if sec[4] else (b, s, sec[1]), BF16)
                   for sec in sections],
        compiler_params=_cparams(2),
        name="inproj",
    )(*args)
    return outs


def _attn_kernel(*refs, l, local, nb):
    if local:
        qt_ref, kp_ref, km_ref, kn_ref, vp_ref, vm_ref, vn_ref, kc_ref, vc_ref, sink_ref, yt_ref = refs
    else:
        qt_ref, kc_ref, vc_ref, sink_ref, yt_ref = refs
    nsub = qt_ref.shape[1] // BLK
    ty = yt_ref.shape[2]
    grp = N_HEADS // N_KV_HEADS
    gsub = ATTN_HEADS_PER_ITEM
    width = gsub * BLK
    i = pl.program_id(1)
    if local:
        kj = lax.broadcasted_iota(jnp.int32, (BLK, width), 0)
        qi = lax.broadcasted_iota(jnp.int32, (BLK, width), 1) & (BLK - 1)
        band_prev, band_next = kj >= qi, kj <= qi
        edge_prev = kj >= qi + jnp.where(i > 0, 0, BLK)
        edge_next = kj <= qi - jnp.where(i < nb - 1, 0, BLK)
    zeros = jnp.zeros((ROT_HALF, width), BF16)

    def key_value_chunks(j, lanes, rows):
        ones = jnp.ones((HALO, BLK), BF16)
        ext = lambda v: jnp.concatenate([v, ones], axis=0)
        out = [(kc_ref[t * BLK:(t + 1) * BLK, lanes], ext(vc_ref[t, rows, :]), None)
               for t in range(vc_ref.shape[0])]
        if local:
            blk = lambda t: slice(t * BLK, (t + 1) * BLK)
            out.append((kp_ref[:, lanes], ext(vp_ref[rows, :]), edge_prev) if j == 0
                       else (km_ref[blk(j - 1), lanes], ext(vm_ref[j - 1, rows, :]), band_prev))
            out.append((km_ref[blk(j), lanes], ext(vm_ref[j, rows, :]), None))
            out.append((kn_ref[:, lanes], ext(vn_ref[rows, :]), edge_next) if j == nsub - 1
                       else (km_ref[blk(j + 1), lanes], ext(vm_ref[j + 1, rows, :]), band_next))
        return out

    def start(j, kh, g0):
        pair, half = kh // 2, kh % 2
        cols = slice(j * BLK, (j + 1) * BLK)
        def piece(r, hi):
            h = kh * grp + g0 + r
            row0 = (h // 2) * LANES + hi * HEAD_DIM + (h % 2) * ROT_HALF
            return qt_ref[row0:row0 + ROT_HALF, cols]
        q_lo = jnp.concatenate([piece(r, 0) for r in range(gsub)], axis=1)
        q_hi = jnp.concatenate([piece(r, 1) for r in range(gsub)], axis=1)
        q_pad = jnp.concatenate([q_lo, zeros, q_hi, zeros] if half == 0 else [zeros, q_lo, zeros, q_hi], axis=0)
        sink = jnp.concatenate(
            [jnp.full((1, BLK), sink_ref[l, kh * grp + g0 + r] * LOG2E, F32) for r in range(gsub)], axis=1)
        chunks = key_value_chunks(j, slice(pair * LANES, (pair + 1) * LANES),
                                  slice(kh * HEAD_DIM, (kh + 1) * HEAD_DIM))
        return dict(q=q_pad, sink=sink, m=sink, acc=jnp.zeros((HEAD_DIM + HALO, width), F32), chunks=chunks)

    def score(st, c):
        k_c, _, mask = st["chunks"][c]
        s_c = jnp.dot(k_c, st["q"], preferred_element_type=F32)
        return s_c if mask is None else jnp.where(mask, s_c, NEG)

    def rescale(st, s_c):
        m_new = jnp.maximum(st["m"], jnp.max(s_c, axis=0, keepdims=True))
        alpha = jnp.exp2(st["m"] - m_new)
        st["m"] = m_new
        return jnp.exp2((s_c - m_new).astype(BF16)), alpha

    def accumulate(st, c, p_c, alpha):
        st["acc"] = st["acc"] * alpha + jnp.dot(st["chunks"][c][1], p_c, preferred_element_type=F32)

    def finish(j, kh, g0, st):
        den = st["acc"][HEAD_DIM:HEAD_DIM + 1, :] + jnp.exp2(st["sink"] - st["m"])
        o_t = st["acc"][:HEAD_DIM, :] / den
        for r in range(gsub):
            h = kh * grp + g0 + r
            t0 = j * BLK
            yt_ref[t0 // ty, h * HEAD_DIM:(h + 1) * HEAD_DIM, t0 % ty:t0 % ty + BLK] = (
                o_t[:, r * BLK:(r + 1) * BLK].astype(BF16))

    items = [(j, kh, g0) for j in range(nsub) for kh in range(N_KV_HEADS) for g0 in range(0, grp, gsub)]
    for n0 in range(0, len(items), ATTN_INTERLEAVE):
        group = items[n0:n0 + ATTN_INTERLEAVE]
        states = [start(*it) for it in group]
        for c in range(len(states[0]["chunks"])):
            scored = [score(st, c) for st in states]
            probs = [rescale(st, s_c) for st, s_c in zip(states, scored)]
            for st, (p_c, alpha) in zip(states, probs):
                accumulate(st, c, p_c, alpha)
        for it, st in zip(group, states):
            finish(*it, st)


def _attention(l, q_t, kc, vc_t, sink, ty, k=None, v_t=None):
    b, nt, _, tq = q_t.shape
    s = nt * tq
    m = kc.shape[1]
    local = k is not None
    nsub, nblk = tq // BLK, s // BLK
    prev_b = lambda i: jnp.maximum(i * nsub - 1, 0)
    next_b = lambda i: jnp.minimum((i + 1) * nsub, nblk - 1)
    in_specs = [pl.BlockSpec((None, None, Q_W, tq), lambda bi, i: (bi, i, 0, 0))]
    args = [q_t]
    if local:
        in_specs += [
            pl.BlockSpec((None, BLK, KV_W), lambda bi, i: (bi, prev_b(i), 0)),
            pl.BlockSpec((None, tq, KV_W), lambda bi, i: (bi, i, 0)),
            pl.BlockSpec((None, BLK, KV_W), lambda bi, i: (bi, next_b(i), 0)),
            pl.BlockSpec((None, None, KV_W, BLK), lambda bi, i: (bi, prev_b(i), 0, 0)),
            pl.BlockSpec((None, nsub, KV_W, BLK), lambda bi, i: (bi, i, 0, 0)),
            pl.BlockSpec((None, None, KV_W, BLK), lambda bi, i: (bi, next_b(i), 0, 0)),
        ]
        args += [k, k, k, v_t, v_t, v_t]
    in_specs += [
        pl.BlockSpec((None, m, KV_W), lambda bi, i: (bi, 0, 0)),
        pl.BlockSpec((None, m // BLK, KV_W, BLK), lambda bi, i: (bi, 0, 0, 0)),
        pl.BlockSpec(memory_space=pltpu.SMEM),
    ]
    args += [kc, vc_t, sink]
    return pl.pallas_call(
        functools.partial(_attn_kernel, l=l, local=local, nb=nt),
        grid=(b, nt),
        in_specs=in_specs,
        out_specs=pl.BlockSpec((None, tq // ty, Q_W, ty), lambda bi, i: (bi, i, 0, 0)),
        out_shape=jax.ShapeDtypeStruct((b, s // ty, Q_W, ty), BF16),
        compiler_params=_cparams(2),
        name="attention",
    )(*args)


def _mix_kernel(x_ref, xp_ref, xn_ref, mod_ref, g_ref, yt_ref, w_ref, ws_ref, bs_ref, gv_ref, wc_ref,
                bg_ref, wb_ref, wo_ref, o_ref, ya_ref, *, nt):
    tm = x_ref.shape[0]
    i = pl.program_id(1)
    sh, sc = mod_ref[:, 0:D_MODEL], mod_ref[:, D_MODEL:2 * D_MODEL]

    def hmod(x):
        return ((_rms(x) * g_ref[...]) * (1.0 + sc) + sh).astype(BF16)

    x = x_ref[...]
    hm = hmod(x)
    he = jnp.concatenate([hmod(xp_ref[...]), hm, hmod(xn_ref[...])], axis=0)
    y_attn = lax.dot_general(yt_ref[...], wb_ref[0], (((0,), (0,)), ((), ())), preferred_element_type=F32)
    z_a = jnp.dot(hm, w_ref[:, MIX_A:MIX_BG], preferred_element_type=F32)
    z_p = jnp.dot(he, w_ref[:, MIX_P:MIX_G], preferred_element_type=F32)
    z_bg = jnp.dot(hm, w_ref[:, MIX_BG:MIX_P], preferred_element_type=F32)
    for n in range(tm // CHUNK):
        rows = slice(n * CHUNK, (n + 1) * CHUNK)
        z = jax.nn.gelu(z_a[rows, :])
        u, v = z[:, :A_W], z[:, A_W:]
        vn = (_rms(v) * gv_ref[...]).astype(BF16)
        for g in range(A_GROUPS):
            cols = slice(g * A_GW, (g + 1) * A_GW)
            mixed = jnp.dot(ws_ref[g], vn[:, cols], preferred_element_type=F32) + bs_ref[:, cols]
            ya_ref[rows, cols] = (u[:, cols] * mixed).astype(BF16)
    z_g = jnp.dot(hm, w_ref[:, MIX_G:], preferred_element_type=F32)
    prod = z_p[:, :B_W] * z_p[:, B_W:]
    r = lax.broadcasted_iota(jnp.int32, (tm, B_W), 0)
    first = r < jnp.where(i == 0, 1, 0)
    last = r >= tm - jnp.where(i == nt - 1, 1, 0)
    pm1 = jnp.where(first, 0.0, pltpu.roll(prod, 1, 0)[HALO:HALO + tm])
    pp1 = jnp.where(last, 0.0, pltpu.roll(prod, tm + 2 * HALO - 1, 0)[HALO:HALO + tm])
    y_b = z_bg * (wc_ref[0:1, :] * pm1 + wc_ref[1:2, :] * prod[HALO:HALO + tm] + wc_ref[2:3, :] * pp1)
    p_a = jnp.dot(ya_ref[...], wb_ref[1], preferred_element_type=F32)
    gates = jax.nn.sigmoid(z_g + bg_ref[...])
    p_b = jnp.dot(y_b.astype(BF16), wb_ref[2], preferred_element_type=F32)
    merged = (gates[:, :D_MODEL] * y_attn + gates[:, D_MODEL:2 * D_MODEL] * p_a + gates[:, 2 * D_MODEL:] * p_b)
    out = jnp.dot(merged.astype(BF16), wo_ref[...], preferred_element_type=F32)
    o_ref[...] = x + mod_ref[:, 2 * D_MODEL:3 * D_MODEL] * out


def _mix(l, x, mod, mod_row, y_t, g, w_abg, ws, bs, gv, wc, bg, wb, wo, tm):
    b, s, _ = x.shape
    nt = s // tm
    params = (w_abg, ws, bs, gv, wc, bg, wb, wo)
    in_specs = [pl.BlockSpec((None, tm, D_MODEL), lambda bi, i: (bi, i, 0))] + _halo_specs(tm, s) + [
        _mod_spec(l, mod_row),
        _layer_spec(g, l),
        pl.BlockSpec((None, None, Q_W, tm), lambda bi, i: (bi, i, 0, 0)),
    ] + [_layer_spec(a, l) for a in params]
    return pl.pallas_call(
        functools.partial(_mix_kernel, nt=nt),
        grid=(b, nt),
        in_specs=in_specs,
        out_specs=pl.BlockSpec((None, tm, D_MODEL), lambda bi, i: (bi, i, 0)),
        out_shape=jax.ShapeDtypeStruct((b, s, D_MODEL), F32),
        scratch_shapes=[pltpu.VMEM((tm, A_W), BF16)],
        compiler_params=_cparams(2),
        name="mix",
    )(x, x, x, mod, g, y_t, *params)


def _ffn_kernel(x_ref, xp_ref, xn_ref, mod_ref, g_ref, wu_ref, wc_ref, wd_ref, gf_ref, o_ref, act_ref,
                *, nt, final, fc):
    tm = x_ref.shape[0]
    i = pl.program_id(1)
    sh, sc = mod_ref[:, 3 * D_MODEL:4 * D_MODEL], mod_ref[:, 4 * D_MODEL:5 * D_MODEL]

    def hmod(x):
        return ((_rms(x) * g_ref[...]) * (1.0 + sc) + sh).astype(BF16)

    x = x_ref[...]
    hm = hmod(x)
    he = jnp.concatenate([hmod(xp_ref[...]), hm, hmod(xn_ref[...])], axis=0)
    r = lax.broadcasted_iota(jnp.int32, (tm, fc), 0)
    first = r < jnp.where(i == 0, 1, 0)
    last = r >= tm - jnp.where(i == nt - 1, 1, 0)
    for c in range(0, D_FF, fc):
        a = jnp.dot(he, wu_ref[:, c:c + fc], preferred_element_type=F32)
        gt = jnp.dot(hm, wu_ref[:, D_FF + c:D_FF + c + fc], preferred_element_type=F32)
        am1 = jnp.where(first, 0.0, pltpu.roll(a, 1, 0)[HALO:HALO + tm])
        ap1 = jnp.where(last, 0.0, pltpu.roll(a, tm + 2 * HALO - 1, 0)[HALO:HALO + tm])
        conv = (wc_ref[0:1, c:c + fc] * am1 + wc_ref[1:2, c:c + fc] * a[HALO:HALO + tm]
                + wc_ref[2:3, c:c + fc] * ap1)
        act_ref[:, c:c + fc] = (jax.nn.silu(conv) * gt).astype(BF16)
    y = jnp.dot(act_ref[...], wd_ref[...], preferred_element_type=F32)
    out = x + mod_ref[:, 5 * D_MODEL:6 * D_MODEL] * y
    if final:
        out = _rms(out) * gf_ref[...]
    o_ref[...] = out


def _ffn(l, x, mod, mod_row, g, wu, wc, wd, gf, tm, final):
    b, s, _ = x.shape
    nt = s // tm
    in_specs = [pl.BlockSpec((None, tm, D_MODEL), lambda bi, i: (bi, i, 0))] + _halo_specs(tm, s) + [
        _mod_spec(l, mod_row),
        _layer_spec(g, l), _layer_spec(wu, l), _layer_spec(wc, l), _layer_spec(wd, l),
        pl.BlockSpec(gf.shape, lambda bi, i: (0, 0)),
    ]
    return pl.pallas_call(
        functools.partial(_ffn_kernel, nt=nt, final=final, fc=256),
        grid=(b, nt),
        in_specs=in_specs,
        out_specs=pl.BlockSpec((None, tm, D_MODEL), lambda bi, i: (bi, i, 0)),
        out_shape=jax.ShapeDtypeStruct((b, s, D_MODEL), F32),
        scratch_shapes=[pltpu.VMEM((tm, D_FF), BF16)],
        compiler_params=_cparams(2),
        name="ffn",
    )(x, x, x, mod, g, wu, wc, wd, gf)


def _rope_tables(n):
    rows = n // GRID_W
    row = jnp.broadcast_to(jnp.arange(rows, dtype=F32)[:, None], (rows, GRID_W)).reshape(n)
    col = jnp.broadcast_to(jnp.arange(GRID_W, dtype=F32)[None, :], (rows, GRID_W)).reshape(n)
    half = HEAD_DIM // 2
    inv = ROPE_THETA ** (-jnp.arange(0, half, 2, dtype=F32) / half)
    ang = jnp.concatenate([row[:, None] * inv, col[:, None] * inv], axis=-1)
    cos, sin = jnp.cos(ang), jnp.sin(ang)
    cos_t = jnp.concatenate([cos, cos, cos, cos], axis=-1)
    sin_t = jnp.concatenate([-sin, -sin, sin, sin], axis=-1)
    return cos_t, sin_t


def _pair_interleave(w):
    lead = w.shape[:-1]
    w = w.reshape(lead + (-1, 2, 2, ROT_HALF))
    return jnp.swapaxes(w, -3, -2).reshape(lead + (-1,))


def kernel(x, c, ctx, c_ctx, w_mod, b_mod, g_mix, w_in, b_gate, sink, w_spatial, b_spatial, g_v, w_sconv,
           w_branch, w_out, g_ffn, w_up, w_fconv, w_down, g_final):
    b, n, _ = x.shape
    m = ctx.shape[1]
    assert b + 1 <= MOD_ROWS and n % 512 == 0 and m % CHUNK == 0

    c_all = jnp.zeros((MOD_ROWS, D_MODEL), F32).at[:b].set(c).at[CTX_ROW].set(c_ctx)
    mod = _modulation(c_all, w_mod, b_mod).reshape(DEPTH, MOD_ROWS, 1, 6 * D_MODEL)
    rope_tabs = _rope_tables(n)

    row = lambda a: a.reshape(DEPTH, 1, -1)
    g_mix_r = row(g_mix)
    w_in_b = w_in.astype(BF16)
    w_qkv_b = jnp.concatenate([_pair_interleave(w_in_b[..., OFF_Q:OFF_K]), _pair_interleave(w_in_b[..., OFF_K:OFF_V]),
                               w_in_b[..., OFF_V:OFF_A]], axis=-1)
    bs_full = jnp.repeat(jnp.transpose(b_spatial, (0, 2, 1)), A_GW, axis=2)
    mix_w = (g_mix_r, w_in_b, w_spatial.astype(BF16), bs_full, row(g_v), w_sconv,
             row(b_gate), w_branch.astype(BF16), w_out.astype(BF16))
    ffn_w = (row(g_ffn), w_up.astype(BF16), w_fconv, w_down.astype(BF16), g_final.reshape(1, D_MODEL))

    lat_row = lambda bi: bi
    ctx_row = lambda bi: CTX_ROW
    tq, tmix, tffn = 1024, 512, 1024
    qkv_sections = lambda t: ((OFF_Q, Q_W, Q_W, QK_SCALE, t), (OFF_K, KV_W, KV_W, 1.0, 0),
                              (OFF_V, KV_W, 0, 1.0, BLK))

    xc = ctx
    for l in range(DEPTH):
        last = l == DEPTH - 1
        q_t, k, v_t = _inproj(l, x, mod, lat_row, g_mix_r, w_qkv_b, qkv_sections(tq), tq, rope_tabs)
        ctx_out = _inproj(l, xc, mod, ctx_row, g_mix_r, w_qkv_b, qkv_sections(m)[1 if last else 0:], m)
        kc, vc_t = ctx_out[-2:]
        y_t = _attention(l, q_t, kc, vc_t, sink, tmix, k=k, v_t=v_t)
        x = _mix(l, x, mod, lat_row, y_t, *mix_w, tm=tmix)
        x = _ffn(l, x, mod, lat_row, *ffn_w, tm=tffn, final=last)
        if not last:
            yc_t = _attention(l, ctx_out[0], kc, vc_t, sink, m)
            xc = _mix(l, xc, mod, ctx_row, yc_t, *mix_w, tm=m)
            xc = _ffn(l, xc, mod, ctx_row, *ffn_w, tm=m, final=False)
    return x
```
